```python
import jax, jax.numpy as jnp
from jax import lax
import numpy as np

D_MODEL = 4096
BATCH = 1
SEQ = 16384
DEPTH = 4

GRID_W = 64
CTX_LEN = 256
N_MIXERS = 4
HEAD_DIM = 128
N_HEADS = D_MODEL // HEAD_DIM
N_KV_HEADS = N_HEADS // 4
GQA_REP = N_HEADS // N_KV_HEADS
WINDOW = 128
BLOCK = 128
Q_LORA = D_MODEL // 4
KV_LORA = D_MODEL // 8
QK_NOPE = 128
QK_ROPE = 64
V_HEAD = 128
CONV_W = 3
D_FF = (5 * D_MODEL) // 4
N_EXPERTS = 8
TOP_K = 2
D_FF_EXPERT = D_MODEL // 4
EPS = 1e-6
ROPE_THETA = 10000.0
NEG_INF = -1e30

kernel_name = 'hybrid_interleaved_diffusion_backbone'


def rms_norm(x, g):
    xf = x.astype(jnp.float32)
    y = xf * lax.rsqrt(jnp.mean(xf * xf, axis=-1, keepdims=True) + EPS)
    return (y * g.astype(jnp.float32)).astype(x.dtype)


def modulate(h, g, shift, scale):
    return rms_norm(h, g) * (1 + scale) + shift


def axial_rope(row, col, rot_dim):
    n_freq = rot_dim // 4
    inv = ROPE_THETA ** (-jnp.arange(n_freq, dtype=jnp.float32) / n_freq)
    ang = jnp.concatenate([row[:, None].astype(jnp.float32) * inv[None, :],
                           col[:, None].astype(jnp.float32) * inv[None, :]], axis=-1)
    return jnp.cos(ang), jnp.sin(ang)


def apply_rope(x, cos, sin):
    xp = x.reshape(x.shape[:-1] + (x.shape[-1] // 2, 2)).astype(jnp.float32)
    x1, x2 = xp[..., 0], xp[..., 1]
    bshape = (1, cos.shape[0]) + (1,) * (x.ndim - 3) + (cos.shape[1],)
    c = cos.reshape(bshape)
    s = sin.reshape(bshape)
    out = jnp.stack([x1 * c - x2 * s, x1 * s + x2 * c], axis=-1)
    return out.reshape(x.shape).astype(x.dtype)


def attend(q, k, v, scale, sink=None, mask=None):
    s = jnp.einsum('bqgrd,bkgd->bgrqk', q, k, preferred_element_type=jnp.float32) * scale
    if mask is not None:
        s = jnp.where(mask, s, NEG_INF)
    if sink is None:
        p = jax.nn.softmax(s, axis=-1)
    else:
        sk = sink.astype(jnp.float32)[None, :, :, None, None]
        m = jnp.maximum(jnp.max(s, axis=-1, keepdims=True), sk)
        e = jnp.exp(s - m)
        p = e / (jnp.sum(e, axis=-1, keepdims=True) + jnp.exp(sk - m))
    return jnp.einsum('bgrqk,bkgd->bqgrd', p.astype(v.dtype), v)


def blocked_dense_attention(q, k_lat, v_lat, k_ctx, v_ctx, scale):
    b, s, g, r, dk = q.shape
    nb = s // BLOCK
    k_all = jnp.concatenate([k_ctx, k_lat], axis=1)
    v_all = jnp.concatenate([v_ctx, v_lat], axis=1)
    qb = jnp.moveaxis(q.reshape(b, nb, BLOCK, g, r, dk), 1, 0)
    out = lax.map(lambda qi: attend(qi, k_all, v_all, scale), qb)
    return jnp.moveaxis(out, 0, 1).reshape(b, s, g, r, -1)


def gqa_project(u, w_in):
    b, n, _ = u.shape
    q, k, v = jnp.split(u @ w_in, [N_HEADS * HEAD_DIM, (N_HEADS + N_KV_HEADS) * HEAD_DIM], axis=-1)
    return (q.reshape(b, n, N_KV_HEADS, GQA_REP, HEAD_DIM),
            k.reshape(b, n, N_KV_HEADS, HEAD_DIM),
            v.reshape(b, n, N_KV_HEADS, HEAD_DIM))


def window_gqa_mixer(u_lat, u_ctx, w_in, sink, w_out, rope, with_ctx_out):
    b, s, _ = u_lat.shape
    l = u_ctx.shape[1]
    nb = s // BLOCK
    scale = HEAD_DIM ** -0.5
    sink = sink.reshape(N_KV_HEADS, GQA_REP)
    q, k, v = gqa_project(u_lat, w_in)
    q = apply_rope(q, *rope)
    k = apply_rope(k, *rope)
    qc, kc, vc = gqa_project(u_ctx, w_in)
    pad = ((0, 0), (BLOCK, BLOCK), (0, 0), (0, 0))
    kb = jnp.pad(k, pad).reshape(b, nb + 2, BLOCK, N_KV_HEADS, HEAD_DIM)
    vb = jnp.pad(v, pad).reshape(b, nb + 2, BLOCK, N_KV_HEADS, HEAD_DIM)
    kw = jnp.concatenate([kb[:, :-2], kb[:, 1:-1], kb[:, 2:]], axis=2)
    vw = jnp.concatenate([vb[:, :-2], vb[:, 1:-1], vb[:, 2:]], axis=2)
    qpos = jnp.arange(s).reshape(nb, BLOCK)
    kpos = jnp.arange(-BLOCK, s + BLOCK).reshape(nb + 2, BLOCK)
    kwin = jnp.concatenate([kpos[:-2], kpos[1:-1], kpos[2:]], axis=1)
    rel = kwin[:, None, :] - qpos[:, :, None]
    valid = (jnp.abs(rel) <= WINDOW) & (kwin[:, None, :] >= 0) & (kwin[:, None, :] < s)
    valid = jnp.concatenate([jnp.ones((nb, BLOCK, l), dtype=bool), valid], axis=-1)

    def one_block(args):
        qi, ki, vi, mi = args
        return attend(qi, jnp.concatenate([kc, ki], axis=1), jnp.concatenate([vc, vi], axis=1),
                      scale, sink=sink, mask=mi)

    qb = jnp.moveaxis(q.reshape(b, nb, BLOCK, N_KV_HEADS, GQA_REP, HEAD_DIM), 1, 0)
    o = lax.map(one_block, (qb, jnp.moveaxis(kw, 1, 0), jnp.moveaxis(vw, 1, 0), valid))
    y_lat = jnp.moveaxis(o, 0, 1).reshape(b, s, N_HEADS * HEAD_DIM) @ w_out
    y_ctx = None
    if with_ctx_out:
        y_ctx = attend(qc, kc, vc, scale, sink=sink).reshape(b, l, N_HEADS * HEAD_DIM) @ w_out
    return y_lat, y_ctx


def mla_project(u, w_in, g_q, g_kv, w_uq, w_ukv, rope):
    b, n, _ = u.shape
    cq, ckv, kr = jnp.split(u @ w_in, [Q_LORA, Q_LORA + KV_LORA], axis=-1)
    q = (rms_norm(cq, g_q) @ w_uq).reshape(b, n, N_HEADS, QK_NOPE + QK_ROPE)
    kv = (rms_norm(ckv, g_kv) @ w_ukv).reshape(b, n, N_HEADS, QK_NOPE + V_HEAD)
    q_nope, q_rot = jnp.split(q, [QK_NOPE], axis=-1)
    k_nope, v = jnp.split(kv, [QK_NOPE], axis=-1)
    kr = kr[:, :, None, :]
    if rope is not None:
        q_rot = apply_rope(q_rot, *rope)
        kr = apply_rope(kr, *rope)
    q = jnp.concatenate([q_nope, q_rot], axis=-1)[:, :, :, None, :]
    k = jnp.concatenate([k_nope, jnp.broadcast_to(kr, (b, n, N_HEADS, QK_ROPE))], axis=-1)
    return q, k, v


def mla_mixer(u_lat, u_ctx, w_in, g_q, g_kv, w_uq, w_ukv, w_out, rope, with_ctx_out):
    b, s, _ = u_lat.shape
    l = u_ctx.shape[1]
    scale = (QK_NOPE + QK_ROPE) ** -0.5
    q, k, v = mla_project(u_lat, w_in, g_q, g_kv, w_uq, w_ukv, rope)
    qc, kc, vc = mla_project(u_ctx, w_in, g_q, g_kv, w_uq, w_ukv, None)
    o = blocked_dense_attention(q, k, v, kc, vc, scale)
    y_lat = o.reshape(b, s, N_HEADS * V_HEAD) @ w_out
    y_ctx = None
    if with_ctx_out:
        y_ctx = attend(qc, kc, vc, scale).reshape(b, l, N_HEADS * V_HEAD) @ w_out
    return y_lat, y_ctx


def qknorm_gqa_mixer(u_lat, u_ctx, w_in, g_q, g_k, w_out, rope, with_ctx_out):
    b, s, _ = u_lat.shape
    l = u_ctx.shape[1]
    scale = HEAD_DIM ** -0.5
    q, k, v = gqa_project(u_lat, w_in)
    q = apply_rope(rms_norm(q, g_q), *rope)
    k = apply_rope(rms_norm(k, g_k), *rope)
    qc, kc, vc = gqa_project(u_ctx, w_in)
    qc = rms_norm(qc, g_q)
    kc = rms_norm(kc, g_k)
    o = blocked_dense_attention(q, k, v, kc, vc, scale)
    y_lat = o.reshape(b, s, N_HEADS * HEAD_DIM) @ w_out
    y_ctx = None
    if with_ctx_out:
        y_ctx = attend(qc, kc, vc, scale).reshape(b, l, N_HEADS * HEAD_DIM) @ w_out
    return y_lat, y_ctx


def short_conv_mixer(u, w_in, conv_w, w_out):
    bg, cg, v = jnp.split(u @ w_in, 3, axis=-1)
    z = lax.conv_general_dilated(cg * v, conv_w[:, None, :], window_strides=(1,),
                                 padding=((CONV_W // 2, CONV_W // 2),),
                                 dimension_numbers=('NWC', 'WIO', 'NWC'),
                                 feature_group_count=D_MODEL)
    return (bg * z) @ w_out


def swiglu(u, w_in, w_out):
    g, up = jnp.split(u @ w_in, 2, axis=-1)
    return (jax.nn.silu(g) * up) @ w_out


def moe_swiglu(u, router, w_in, w_out):
    logits = (u @ router).astype(jnp.float32)
    top_val, top_idx = lax.top_k(logits, TOP_K)
    gate = jax.nn.softmax(top_val, axis=-1)
    combine = jnp.einsum('bnk,bnke->bne', gate,
                         jax.nn.one_hot(top_idx, N_EXPERTS, dtype=jnp.float32)).astype(u.dtype)
    gu = jnp.einsum('bnd,edf->bnef', u, w_in)
    g, up = jnp.split(gu, 2, axis=-1)
    act = jax.nn.silu(g) * up * combine[..., None]
    return jnp.einsum('bnef,efd->bnd', act, w_out)


def setup_inputs(seed: int = 0) -> dict:
    key = jax.random.key(seed)
    keys = jax.random.split(key, 32)
    counter = [0]

    def nrm(shape, scale=1.0):
        k = keys[counter[0]]
        counter[0] += 1
        return jax.random.normal(k, shape, jnp.float32) * scale

    def gain(shape):
        return 1.0 + 0.02 * nrm(shape)

    d = D_MODEL
    n_win = len(range(0, DEPTH, N_MIXERS))
    n_mla = len(range(1, DEPTH, N_MIXERS))
    n_qkn = len(range(2, DEPTH, N_MIXERS))
    n_conv = len(range(3, DEPTH, N_MIXERS))
    n_dense = (DEPTH + 1) // 2
    n_moe = DEPTH // 2
    gqa_cols = (N_HEADS + 2 * N_KV_HEADS) * HEAD_DIM
    return {
        'x': nrm((BATCH, SEQ, d)),
        'c': nrm((BATCH, d)),
        'ctx': nrm((BATCH, CTX_LEN, d)),
        'c_ctx': nrm((d,)),
        'ada_w': nrm((DEPTH, d, 6 * d), 0.5 * d ** -0.5),
        'ada_b': nrm((DEPTH, 6 * d), 0.01),
        'g_mix_pre': gain((DEPTH, d)),
        'g_mix_post': gain((DEPTH, d)),
        'g_ffn_pre': gain((DEPTH, d)),
        'g_ffn_post': gain((DEPTH, d)),
        'win_w_in': nrm((n_win, d, gqa_cols), d ** -0.5),
        'win_sink': nrm((n_win, N_HEADS), 1.0),
        'win_w_out': nrm((n_win, N_HEADS * HEAD_DIM, d), (N_HEADS * HEAD_DIM) ** -0.5),
        'mla_w_in': nrm((n_mla, d, Q_LORA + KV_LORA + QK_ROPE), d ** -0.5),
        'mla_g_q': gain((n_mla, Q_LORA)),
        'mla_g_kv': gain((n_mla, KV_LORA)),
        'mla_w_uq': nrm((n_mla, Q_LORA, N_HEADS * (QK_NOPE + QK_ROPE)), Q_LORA ** -0.5),
        'mla_w_ukv': nrm((n_mla, KV_LORA, N_HEADS * (QK_NOPE + V_HEAD)), KV_LORA ** -0.5),
        'mla_w_out': nrm((n_mla, N_HEADS * V_HEAD, d), (N_HEADS * V_HEAD) ** -0.5),
        'qkn_w_in': nrm((n_qkn, d, gqa_cols), d ** -0.5),
        'qkn_g_q': gain((n_qkn, HEAD_DIM)),
        'qkn_g_k': gain((n_qkn, HEAD_DIM)),
        'qkn_w_out': nrm((n_qkn, N_HEADS * HEAD_DIM, d), (N_HEADS * HEAD_DIM) ** -0.5),
        'conv_w_in': nrm((n_conv, d, 3 * d), d ** -0.5),
        'conv_w': nrm((n_conv, CONV_W, d), CONV_W ** -0.5),
        'conv_w_out': nrm((n_conv, d, d), d ** -0.5),
        'ffn_w_in': nrm((n_dense, d, 2 * D_FF), d ** -0.5),
        'ffn_w_out': nrm((n_dense, D_FF, d), D_FF ** -0.5),
        'moe_router': nrm((n_moe, d, N_EXPERTS), d ** -0.5),
        'moe_w_in': nrm((n_moe, N_EXPERTS, d, 2 * D_FF_EXPERT), d ** -0.5),
        'moe_w_out': nrm((n_moe, N_EXPERTS, D_FF_EXPERT, d), D_FF_EXPERT ** -0.5),
    }


def reference(x, c, ctx, c_ctx, ada_w, ada_b, g_mix_pre, g_mix_post, g_ffn_pre, g_ffn_post,
              win_w_in, win_sink, win_w_out,
              mla_w_in, mla_g_q, mla_g_kv, mla_w_uq, mla_w_ukv, mla_w_out,
              qkn_w_in, qkn_g_q, qkn_g_k, qkn_w_out,
              conv_w_in, conv_w, conv_w_out,
              ffn_w_in, ffn_w_out, moe_router, moe_w_in, moe_w_out):
    b, s, _ = x.shape
    rows = s // GRID_W
    row = jnp.repeat(jnp.arange(rows, dtype=jnp.int32), GRID_W)
    col = jnp.tile(jnp.arange(GRID_W, dtype=jnp.int32), rows)
    rope_head = axial_rope(row, col, HEAD_DIM)
    rope_mla = axial_rope(row, col, QK_ROPE)

    h, hc = x, ctx
    for i in range(DEPTH):
        kind = i % N_MIXERS
        j = i // N_MIXERS
        f = i // 2
        last = i == DEPTH - 1
        mod_l = jnp.split((jax.nn.silu(c) @ ada_w[i] + ada_b[i])[:, None, :], 6, axis=-1)
        mod_c = jnp.split(jax.nn.silu(c_ctx) @ ada_w[i] + ada_b[i], 6, axis=-1)
        ctx_side = (not last) or kind != 3

        u_l = modulate(h, g_mix_pre[i], mod_l[0], mod_l[1])
        u_c = modulate(hc, g_mix_pre[i], mod_c[0], mod_c[1]) if ctx_side else None
        if kind == 0:
            y_l, y_c = window_gqa_mixer(u_l, u_c, win_w_in[j], win_sink[j], win_w_out[j],
                                        rope_head, not last)
        elif kind == 1:
            y_l, y_c = mla_mixer(u_l, u_c, mla_w_in[j], mla_g_q[j], mla_g_kv[j], mla_w_uq[j],
                                 mla_w_ukv[j], mla_w_out[j], rope_mla, not last)
        elif kind == 2:
            y_l, y_c = qknorm_gqa_mixer(u_l, u_c, qkn_w_in[j], qkn_g_q[j], qkn_g_k[j], qkn_w_out[j],
                                        rope_head, not last)
        else:
            y_l = short_conv_mixer(u_l, conv_w_in[j], conv_w[j], conv_w_out[j])
            y_c = None if last else short_conv_mixer(u_c, conv_w_in[j], conv_w[j], conv_w_out[j])
        h = h + mod_l[2] * rms_norm(y_l, g_mix_post[i])
        if not last:
            hc = hc + mod_c[2] * rms_norm(y_c, g_mix_post[i])

        u_l = modulate(h, g_ffn_pre[i], mod_l[3], mod_l[4])
        if i % 2 == 0:
            z_l = swiglu(u_l, ffn_w_in[f], ffn_w_out[f])
        else:
            z_l = moe_swiglu(u_l, moe_router[f], moe_w_in[f], moe_w_out[f])
        h = h + mod_l[5] * rms_norm(z_l, g_ffn_post[i])
        if not last:
            u_c = modulate(hc, g_ffn_pre[i], mod_c[3], mod_c[4])
            if i % 2 == 0:
                z_c = swiglu(u_c, ffn_w_in[f], ffn_w_out[f])
            else:
                z_c = moe_swiglu(u_c, moe_router[f], moe_w_in[f], moe_w_out[f])
            hc = hc + mod_c[5] * rms_norm(z_c, g_ffn_post[i])
    return h
```

```python
import functools
import math

import jax
import jax.numpy as jnp
import numpy as np
from jax import lax
from jax.experimental import pallas as pl
from jax.experimental.pallas import tpu as pltpu

HEAD_DIM = 128
GRID_W = 64
QK_NOPE = 128
QK_ROPE = 64
V_HEAD = 128
WINDOW = 128
EPS = 1e-6
ROPE_THETA = 10000.0
NEG_INF = -1e30
LOG2E = 1.4426950408889634
LANES = 128
ROW_TILE = 256
VMEM_LIMIT = 52 * 1024 * 1024

F32 = jnp.float32
BF16 = jnp.bfloat16


def _pick(n, candidates):
    for c in candidates:
        if n % c == 0:
            return c
    raise ValueError(f"no tile for {n} in {candidates}")


def _params(sem):
    return pltpu.CompilerParams(dimension_semantics=sem, vmem_limit_bytes=VMEM_LIMIT)


def _dot(a, b):
    return jnp.dot(a, b, preferred_element_type=F32)


def _silu(x):
    return x / (1.0 + jnp.exp(-x))


def _rms(x, g):
    return x * lax.rsqrt(jnp.mean(x * x, axis=-1, keepdims=True) + EPS) * g


def _ada_kernel(cc_ref, w_ref, b_ref, o_ref):
    a = _silu(cc_ref[...]).astype(BF16)
    o_ref[0] = _dot(a, w_ref[0].astype(BF16)) + b_ref[0]


def _ada(cc, ada_w, ada_b):
    depth, d, n = ada_w.shape
    tn = _pick(n, (1024, 512, 256, 128))
    return pl.pallas_call(
        _ada_kernel,
        grid=(depth, n // tn),
        in_specs=[pl.BlockSpec((8, d), lambda l, j: (0, 0)),
                  pl.BlockSpec((1, d, tn), lambda l, j: (l, 0, j)),
                  pl.BlockSpec((1, 1, tn), lambda l, j: (l, 0, j))],
        out_specs=pl.BlockSpec((1, 8, tn), lambda l, j: (l, 0, j)),
        out_shape=jax.ShapeDtypeStruct((depth, 8, n), F32),
        compiler_params=_params(("parallel", "parallel")),
    )(cc, ada_w, ada_b.reshape(depth, 1, n))


def _mod_vec(mod_ref, is_ctx, k, d):
    return jnp.where(is_ctx, mod_ref[1:2, k * d:(k + 1) * d], mod_ref[0:1, k * d:(k + 1) * d])


def _modulate_kernel(h_ref, g_ref, mod_ref, u_ref, *, ctx_tiles, k0):
    d = h_ref.shape[-1]
    is_ctx = pl.program_id(0) < ctx_tiles
    shift = _mod_vec(mod_ref, is_ctx, k0, d)
    scale = _mod_vec(mod_ref, is_ctx, k0 + 1, d)
    u_ref[...] = (_rms(h_ref[...], g_ref[...]) * (1.0 + scale) + shift).astype(BF16)


def _modulate(h, g, mod, k0, ctx_rows):
    t, d = h.shape
    return pl.pallas_call(
        functools.partial(_modulate_kernel, ctx_tiles=ctx_rows // ROW_TILE, k0=k0),
        grid=(t // ROW_TILE,),
        in_specs=[pl.BlockSpec((ROW_TILE, d), lambda i: (i, 0)),
                  pl.BlockSpec((1, d), lambda i: (0, 0)),
                  pl.BlockSpec((8, 6 * d), lambda i: (0, 0))],
        out_specs=pl.BlockSpec((ROW_TILE, d), lambda i: (i, 0)),
        out_shape=jax.ShapeDtypeStruct((t, d), BF16),
        compiler_params=_params(("parallel",)),
    )(h, g.reshape(1, d), mod)


def _top2_combine(logits, n_experts):
    lane = lax.broadcasted_iota(jnp.int32, logits.shape, 1)
    lg = jnp.where(lane < n_experts, logits, -jnp.inf)
    m1 = jnp.max(lg, axis=1, keepdims=True)
    i1 = jnp.min(jnp.where(lg == m1, lane, LANES), axis=1, keepdims=True)
    lg2 = jnp.where(lane == i1, -jnp.inf, lg)
    m2 = jnp.max(lg2, axis=1, keepdims=True)
    i2 = jnp.min(jnp.where(lg2 == m2, lane, LANES), axis=1, keepdims=True)
    e2 = jnp.exp(m2 - m1)
    g1 = 1.0 / (1.0 + e2)
    g2 = e2 / (1.0 + e2)
    return jnp.where(lane == i1, g1, 0.0) + jnp.where(lane == i2, g2, 0.0)


def _resid_kernel(*refs, ctx_tiles, tile_off, gate_k, next_k, n_experts):
    h_ref, y_ref, gp_ref, mod_ref = refs[:4]
    pos = 4
    if next_k is not None:
        gn_ref, modn_ref = refs[pos:pos + 2]
        pos += 2
    if n_experts:
        router_ref = refs[pos]
        pos += 1
    outs = refs[pos:]
    d = h_ref.shape[-1]
    is_ctx = (pl.program_id(0) + tile_off) < ctx_tiles
    gate = _mod_vec(mod_ref, is_ctx, gate_k, d)
    h = h_ref[...] + gate * _rms(y_ref[...].astype(F32), gp_ref[...])
    outs[0][...] = h
    if next_k is not None:
        shift = _mod_vec(modn_ref, is_ctx, next_k, d)
        scale = _mod_vec(modn_ref, is_ctx, next_k + 1, d)
        u = _rms(h, gn_ref[...]) * (1.0 + scale) + shift
        outs[1][...] = u.astype(BF16)
        if n_experts:
            logits = jnp.dot(u, router_ref[...], preferred_element_type=F32,
                             precision=lax.Precision.HIGHEST)
            outs[2][...] = _top2_combine(logits, n_experts)


def _resid(h, y, g_post, mod, gate_k, ctx_rows, nxt=None, router=None, out_rows=None):
    t, d = h.shape
    off = 0 if out_rows is None else (t - out_rows) // ROW_TILE
    n_out = t if out_rows is None else out_rows
    row = lambda i: (i + off, 0)
    const = lambda i: (0, 0)
    args = [h, y, g_post.reshape(1, d), mod]
    in_specs = [pl.BlockSpec((ROW_TILE, d), row), pl.BlockSpec((ROW_TILE, d), row),
                pl.BlockSpec((1, d), const), pl.BlockSpec((8, 6 * d), const)]
    out_shape = [jax.ShapeDtypeStruct((n_out, d), F32)]
    out_specs = [pl.BlockSpec((ROW_TILE, d), lambda i: (i, 0))]
    n_experts = 0
    if nxt is not None:
        g_pre, mod_next, k0 = nxt
        args += [g_pre.reshape(1, d), mod_next]
        in_specs += [pl.BlockSpec((1, d), const), pl.BlockSpec((8, 6 * d), const)]
        out_shape.append(jax.ShapeDtypeStruct((t, d), BF16))
        out_specs.append(pl.BlockSpec((ROW_TILE, d), lambda i: (i, 0)))
        if router is not None:
            n_experts = router.shape[-1]
            args.append(jnp.pad(router, ((0, 0), (0, LANES - n_experts))))
            in_specs.append(pl.BlockSpec((d, LANES), const))
            out_shape.append(jax.ShapeDtypeStruct((t, LANES), F32))
            out_specs.append(pl.BlockSpec((ROW_TILE, LANES), lambda i: (i, 0)))
    return pl.pallas_call(
        functools.partial(_resid_kernel, ctx_tiles=ctx_rows // ROW_TILE, tile_off=off, gate_k=gate_k,
                          next_k=None if nxt is None else nxt[2], n_experts=n_experts),
        grid=(n_out // ROW_TILE,),
        in_specs=in_specs, out_specs=out_specs, out_shape=out_shape,
        input_output_aliases={0: 0} if out_rows is None else {},
        compiler_params=_params(("parallel",)),
    )(*args)


MM_VMEM_BUDGET = 40 * 1024 * 1024


def _row_tile(t, k, tn, n_w=1, out_bytes=2):
    for tm in (1280, 1024, 640, 512, 256, 128):
        if t % tm:
            continue
        need = 2 * tm * k * 2 + n_w * 2 * k * tn * 2 + 2 * tm * tn * out_bytes + n_w * tm * tn * 4
        if need <= MM_VMEM_BUDGET:
            return tm
    return 0


def _mm_tiles(t, k, n, n_w=1, out_bytes=2, tn_max=512):
    best = (0, 0)
    for tn in (512, 256, 128):
        if tn > tn_max or n % tn:
            continue
        tm = _row_tile(t, k, tn, n_w, out_bytes)
        if tm > best[0]:
            best = (tm, tn)
    assert best[0], f"no matmul tiling for t={t} k={k} n={n}"
    return best


def _mm_kernel(a_ref, w_ref, o_ref):
    o_ref[...] = _dot(a_ref[...], w_ref[...]).astype(o_ref.dtype)


def _mm(a, w, out_dtype):
    t, k = a.shape
    n = w.shape[1]
    tm, tn = _mm_tiles(t, k, n, out_bytes=jnp.dtype(out_dtype).itemsize)
    return pl.pallas_call(
        _mm_kernel,
        grid=(t // tm, n // tn),
        in_specs=[pl.BlockSpec((tm, k), lambda i, j: (i, 0)),
                  pl.BlockSpec((k, tn), lambda i, j: (0, j))],
        out_specs=pl.BlockSpec((tm, tn), lambda i, j: (i, j)),
        out_shape=jax.ShapeDtypeStruct((t, n), out_dtype),
        compiler_params=_params(("parallel", "arbitrary")),
    )(a, w)


def _swiglu_kernel(a_ref, wg_ref, wu_ref, o_ref):
    a = a_ref[...]
    g = _dot(a, wg_ref[...])
    o_ref[...] = (_silu(g) * _dot(a, wu_ref[...])).astype(BF16)


def _mm_swiglu(a, w):
    t, k = a.shape
    f = w.shape[1] // 2
    tm, tn = _mm_tiles(t, k, f, n_w=2)
    nt = f // tn
    return pl.pallas_call(
        _swiglu_kernel,
        grid=(t // tm, nt),
        in_specs=[pl.BlockSpec((tm, k), lambda i, j: (i, 0)),
                  pl.BlockSpec((k, tn), lambda i, j: (0, j)),
                  pl.BlockSpec((k, tn), lambda i, j: (0, j + nt))],
        out_specs=pl.BlockSpec((tm, tn), lambda i, j: (i, j)),
        out_shape=jax.ShapeDtypeStruct((t, f), BF16),
        compiler_params=_params(("parallel", "arbitrary")),
    )(a, w, w)


def _moe_in_kernel(a_ref, wg_ref, wu_ref, comb_ref, o_ref, *, tiles_per_expert):
    e = pl.program_id(1) // tiles_per_expert
    a = a_ref[...]
    g = _dot(a, wg_ref[0])
    up = _dot(a, wu_ref[0])
    comb = comb_ref[...]
    lane = lax.broadcasted_iota(jnp.int32, comb.shape, 1)
    c = jnp.sum(jnp.where(lane == e, comb, 0.0), axis=1, keepdims=True)
    o_ref[...] = (_silu(g) * up * c).astype(BF16)


def _moe_in(a, w_in, comb):
    t, k = a.shape
    n_exp, _, f2 = w_in.shape
    f = f2 // 2
    tm, tn = _mm_tiles(t, k, f, n_w=2)
    nt = f // tn
    return pl.pallas_call(
        functools.partial(_moe_in_kernel, tiles_per_expert=nt),
        grid=(t // tm, n_exp * nt),
        in_specs=[pl.BlockSpec((tm, k), lambda i, j: (i, 0)),
                  pl.BlockSpec((1, k, tn), lambda i, j: (j // nt, 0, j % nt)),
                  pl.BlockSpec((1, k, tn), lambda i, j: (j // nt, 0, j % nt + nt)),
                  pl.BlockSpec((tm, LANES), lambda i, j: (i, 0))],
        out_specs=pl.BlockSpec((tm, tn), lambda i, j: (i, j)),
        out_shape=jax.ShapeDtypeStruct((t, n_exp * f), BF16),
        compiler_params=_params(("parallel", "arbitrary")),
    )(a, w_in, w_in, comb)


def _rope128(x, cos2, sin2):
    return x * cos2 + pltpu.roll(x, 64, 1) * sin2


def _rope64(x, cosp, sa, sb):
    return x * cosp + pltpu.roll(x, 32, 1) * sa + pltpu.roll(x, 96, 1) * sb


def _qkv_kernel(a_ref, w_ref, cos_ref, sin_ref, gq_ref, gk_ref, o_ref, *, q_tiles, k_tiles,
                q_scale, qk_norm):
    j = pl.program_id(1)
    acc = _dot(a_ref[...], w_ref[...])

    @pl.when(j < q_tiles + k_tiles)
    def _():
        is_q = j < q_tiles
        cos2, sin2 = cos_ref[...], sin_ref[...]
        g = jnp.where(is_q, gq_ref[...], gk_ref[...])
        sc = jnp.where(is_q, q_scale, 1.0)
        outs = []
        for hh in range(acc.shape[1] // HEAD_DIM):
            x = acc[:, hh * HEAD_DIM:(hh + 1) * HEAD_DIM]
            if qk_norm:
                x = _rms(x, g)
            outs.append((_rope128(x, cos2, sin2) * sc).astype(BF16))
        o_ref[...] = jnp.concatenate(outs, axis=1)

    @pl.when(j >= q_tiles + k_tiles)
    def _():
        o_ref[...] = acc.astype(BF16)


def _mm_qkv(a, w, cos2, sin2, gq, gk, n_heads, n_kv, qk_norm):
    t, k = a.shape
    n = w.shape[1]
    tm, tn = _mm_tiles(t, k, n_kv * HEAD_DIM)
    const = lambda i, j: (0, 0)
    return pl.pallas_call(
        functools.partial(_qkv_kernel, q_tiles=n_heads * HEAD_DIM // tn, k_tiles=n_kv * HEAD_DIM // tn,
                          q_scale=HEAD_DIM ** -0.5 * LOG2E, qk_norm=qk_norm),
        grid=(t // tm, n // tn),
        in_specs=[pl.BlockSpec((tm, k), lambda i, j: (i, 0)),
                  pl.BlockSpec((k, tn), lambda i, j: (0, j)),
                  pl.BlockSpec((tm, HEAD_DIM), lambda i, j: (i, 0)),
                  pl.BlockSpec((tm, HEAD_DIM), lambda i, j: (i, 0)),
                  pl.BlockSpec((1, HEAD_DIM), const),
                  pl.BlockSpec((1, HEAD_DIM), const)],
        out_specs=pl.BlockSpec((tm, tn), lambda i, j: (i, j)),
        out_shape=jax.ShapeDtypeStruct((t, n), BF16),
        compiler_params=_params(("parallel", "arbitrary")),
    )(a, w, cos2, sin2, gq.reshape(1, HEAD_DIM), gk.reshape(1, HEAD_DIM))


def _flash_kernel(kvi_ref, nact_ref, q_ref, k_ref, v_ref, sink_ref, o_ref, qs_ref, m_ref, l_ref,
                  acc_ref, *, rep, dq, dv, tq, tk, ctx_rows, window, use_sink, n_steps):
    g, i, j = pl.program_id(0), pl.program_id(1), pl.program_id(2)

    @pl.when(j == 0)
    def _init():
        for r in range(rep):
            qs_ref[r * tq:(r + 1) * tq, :] = q_ref[:, r * dq:(r + 1) * dq]
            if use_sink:
                m_ref[r * tq:(r + 1) * tq, :] = jnp.full((tq, 1), sink_ref[g * rep + r] * LOG2E, F32)
        if use_sink:
            l_ref[...] = jnp.ones(l_ref.shape, F32)
        else:
            m_ref[...] = jnp.full(m_ref.shape, NEG_INF, F32)
            l_ref[...] = jnp.zeros(l_ref.shape, F32)
        acc_ref[...] = jnp.zeros(acc_ref.shape, F32)

    kb = kvi_ref[i * n_steps + j]

    def scores():
        return lax.dot_general(qs_ref[...], k_ref[...], (((1,), (1,)), ((), ())),
                               preferred_element_type=F32)

    def update(s):
        m_prev = m_ref[...]
        m_new = jnp.maximum(m_prev, jnp.max(s, axis=1, keepdims=True))
        alpha = jnp.exp2(m_prev - m_new)
        p = jnp.exp2(s - m_new)
        l_ref[...] = alpha * l_ref[...] + jnp.sum(p, axis=1, keepdims=True)
        acc_ref[...] = alpha * acc_ref[...] + _dot(p.astype(BF16), v_ref[...])
        m_ref[...] = m_new

    active = j < nact_ref[i]
    has_latent_keys = (kb + 1) * tk > ctx_rows
    need_mask = has_latent_keys if window else jnp.logical_and(has_latent_keys, i * tq < ctx_rows)

    @pl.when(jnp.logical_and(active, need_mask))
    def _masked():
        qpos = i * tq + lax.broadcasted_iota(jnp.int32, (tq, tk), 0)
        kpos = kb * tk + lax.broadcasted_iota(jnp.int32, (tq, tk), 1)
        lat = qpos >= ctx_rows
        if window:
            lat = jnp.logical_and(lat, jnp.abs(kpos - qpos) <= WINDOW)
        valid = jnp.logical_or(kpos < ctx_rows, lat)
        s = scores()
        update(jnp.concatenate(
            [jnp.where(valid, s[r * tq:(r + 1) * tq], NEG_INF) for r in range(rep)], axis=0))

    @pl.when(jnp.logical_and(active, jnp.logical_not(need_mask)))
    def _plain():
        update(scores())

    @pl.when(j == n_steps - 1)
    def _fin():
        o = acc_ref[...] / l_ref[...]
        for r in range(rep):
            o_ref[:, r * dv:(r + 1) * dv] = o[r * tq:(r + 1) * tq].astype(BF16)


def _kv_schedule(t, ctx_rows, tq, tk, window):
    nq, nkv = t // tq, t // tk
    ctx_kv = -(-ctx_rows // tk)
    lists = []
    for i in range(nq):
        if (i + 1) * tq <= ctx_rows:
            lists.append(list(range(ctx_kv)))
        elif window:
            lo = max(ctx_rows, i * tq - WINDOW) // tk
            hi = min(t - 1, (i + 1) * tq - 1 + WINDOW) // tk
            lists.append(sorted(set(range(ctx_kv)) | set(range(lo, hi + 1))))
        else:
            lists.append(list(range(nkv)))
    n_steps = max(len(x) for x in lists)
    kvi = np.array([x + [x[-1]] * (n_steps - len(x)) for x in lists], np.int32).reshape(-1)
    nact = np.array([len(x) for x in lists], np.int32)
    return jnp.asarray(kvi), jnp.asarray(nact), n_steps


def _flash(q_arr, k_arr, v_arr, sink, *, groups, rep, dq, dv, q_off, k_off, v_off, tq, tk,
           ctx_rows, window):
    t = q_arr.shape[0]
    if window:
        assert tq == tk and ctx_rows % tk == 0 and WINDOW <= tk
    kvi, nact, n_steps = _kv_schedule(t, ctx_rows, tq, tk, window)
    use_sink = sink is not None
    if sink is None:
        sink = jnp.zeros((groups * rep,), F32)
    rows = rep * tq
    grid_spec = pltpu.PrefetchScalarGridSpec(
        num_scalar_prefetch=2,
        grid=(groups, t // tq, n_steps),
        in_specs=[pl.BlockSpec((tq, rep * dq), lambda g, i, j, kvi, na: (i, q_off // rep + g)),
                  pl.BlockSpec((tk, dq), lambda g, i, j, kvi, na: (kvi[i * n_steps + j], k_off + g)),
                  pl.BlockSpec((tk, dv), lambda g, i, j, kvi, na: (kvi[i * n_steps + j], v_off + g)),
                  pl.BlockSpec(memory_space=pltpu.SMEM)],
        out_specs=pl.BlockSpec((tq, rep * dv), lambda g, i, j, kvi, na: (i, g)),
        scratch_shapes=[pltpu.VMEM((rows, dq), BF16), pltpu.VMEM((rows, 1), F32),
                        pltpu.VMEM((rows, 1), F32), pltpu.VMEM((rows, dv), F32)])
    return pl.pallas_call(
        functools.partial(_flash_kernel, rep=rep, dq=dq, dv=dv, tq=tq, tk=tk, ctx_rows=ctx_rows,
                          window=window, use_sink=use_sink, n_steps=n_steps),
        grid_spec=grid_spec,
        out_shape=jax.ShapeDtypeStruct((t, groups * rep * dv), BF16),
        compiler_params=_params(("parallel", "parallel", "arbitrary")),
    )(kvi, nact, q_arr, k_arr, v_arr, sink)


def _mla_in_kernel(a_ref, w_ref, gq_ref, gkv_ref, cos_ref, sa_ref, sb_ref, cq_ref, ckv_ref, kr_ref,
                   *, q_lora, kv_lora):
    acc = _dot(a_ref[...], w_ref[...])
    cq_ref[...] = _rms(acc[:, :q_lora], gq_ref[...]).astype(BF16)
    ckv_ref[...] = _rms(acc[:, q_lora:q_lora + kv_lora], gkv_ref[...]).astype(BF16)
    kr = acc[:, q_lora + kv_lora:]
    kr_ref[...] = _rope64(kr, cos_ref[...], sa_ref[...], sb_ref[...]).astype(BF16)


def _mla_in(a, w, g_q, g_kv, tabs):
    t, k = a.shape
    n = w.shape[1]
    q_lora, kv_lora = g_q.shape[0], g_kv.shape[0]
    tm = _pick(t, (256, 128))
    const = lambda i: (0, 0)
    row = lambda i: (i, 0)
    return pl.pallas_call(
        functools.partial(_mla_in_kernel, q_lora=q_lora, kv_lora=kv_lora),
        grid=(t // tm,),
        in_specs=[pl.BlockSpec((tm, k), row), pl.BlockSpec((k, n), const),
                  pl.BlockSpec((1, q_lora), const), pl.BlockSpec((1, kv_lora), const),
                  pl.BlockSpec((tm, LANES), row), pl.BlockSpec((tm, LANES), row),
                  pl.BlockSpec((tm, LANES), row)],
        out_specs=[pl.BlockSpec((tm, q_lora), row), pl.BlockSpec((tm, kv_lora), row),
                   pl.BlockSpec((tm, LANES), row)],
        out_shape=[jax.ShapeDtypeStruct((t, q_lora), BF16), jax.ShapeDtypeStruct((t, kv_lora), BF16),
                   jax.ShapeDtypeStruct((t, LANES), BF16)],
        compiler_params=_params(("parallel",)),
    )(a, w, g_q.reshape(1, -1), g_kv.reshape(1, -1), *tabs)


def _mla_q_kernel(a_ref, w_ref, cos_ref, sa_ref, sb_ref, o_ref, *, q_scale):
    acc = _dot(a_ref[...], w_ref[...])
    cosp, sa, sb = cos_ref[...], sa_ref[...], sb_ref[...]
    outs = []
    for hh in range(acc.shape[1] // 256):
        outs.append((acc[:, hh * 256:hh * 256 + 128] * q_scale).astype(BF16))
        outs.append((_rope64(acc[:, hh * 256 + 128:(hh + 1) * 256], cosp, sa, sb) * q_scale).astype(BF16))
    o_ref[...] = jnp.concatenate(outs, axis=1)


def _mla_q(a, w, tabs):
    t, k = a.shape
    n = w.shape[1]
    tm, tn = _row_tile(t, k, 512), 512
    row = lambda i, j: (i, 0)
    return pl.pallas_call(
        functools.partial(_mla_q_kernel, q_scale=(QK_NOPE + QK_ROPE) ** -0.5 * LOG2E),
        grid=(t // tm, n // tn),
        in_specs=[pl.BlockSpec((tm, k), row), pl.BlockSpec((k, tn), lambda i, j: (0, j)),
                  pl.BlockSpec((tm, LANES), row), pl.BlockSpec((tm, LANES), row),
                  pl.BlockSpec((tm, LANES), row)],
        out_specs=pl.BlockSpec((tm, tn), lambda i, j: (i, j)),
        out_shape=jax.ShapeDtypeStruct((t, n), BF16),
        compiler_params=_params(("parallel", "arbitrary")),
    )(a, w, *tabs)


def _mla_kv_kernel(a_ref, w_ref, kr_ref, k_ref, v_ref):
    acc = _dot(a_ref[...], w_ref[...])
    kr = kr_ref[...]
    ks, vs = [], []
    for hh in range(acc.shape[1] // 256):
        ks += [acc[:, hh * 256:hh * 256 + 128].astype(BF16), kr]
        vs.append(acc[:, hh * 256 + 128:(hh + 1) * 256].astype(BF16))
    k_ref[...] = jnp.concatenate(ks, axis=1)
    v_ref[...] = jnp.concatenate(vs, axis=1) if len(vs) > 1 else vs[0]


def _mla_kv(a, w, kr):
    t, k = a.shape
    n = w.shape[1]
    tm, tn = _row_tile(t, k, 512), 512
    row = lambda i, j: (i, 0)
    return pl.pallas_call(
        _mla_kv_kernel,
        grid=(t // tm, n // tn),
        in_specs=[pl.BlockSpec((tm, k), row), pl.BlockSpec((k, tn), lambda i, j: (0, j)),
                  pl.BlockSpec((tm, LANES), row)],
        out_specs=[pl.BlockSpec((tm, tn), lambda i, j: (i, j)),
                   pl.BlockSpec((tm, tn // 2), lambda i, j: (i, j))],
        out_shape=[jax.ShapeDtypeStruct((t, n), BF16), jax.ShapeDtypeStruct((t, n // 2), BF16)],
        compiler_params=_params(("parallel", "arbitrary")),
    )(a, w, kr)


def _conv_in_kernel(a_ref, wb_ref, wc_ref, wv_ref, b_ref, p_ref):
    a = a_ref[...]
    b_ref[...] = _dot(a, wb_ref[...]).astype(BF16)
    p_ref[...] = (_dot(a, wc_ref[...]) * _dot(a, wv_ref[...])).astype(BF16)


def _conv_in(a, w):
    t, k = a.shape
    d = w.shape[1] // 3
    tm, tn = _mm_tiles(t, k, d, n_w=3, out_bytes=4)
    nt = d // tn
    return pl.pallas_call(
        _conv_in_kernel,
        grid=(t // tm, nt),
        in_specs=[pl.BlockSpec((tm, k), lambda i, j: (i, 0)),
                  pl.BlockSpec((k, tn), lambda i, j: (0, j)),
                  pl.BlockSpec((k, tn), lambda i, j: (0, j + nt)),
                  pl.BlockSpec((k, tn), lambda i, j: (0, j + 2 * nt))],
        out_specs=[pl.BlockSpec((tm, tn), lambda i, j: (i, j))] * 2,
        out_shape=[jax.ShapeDtypeStruct((t, d), BF16)] * 2,
        compiler_params=_params(("parallel", "arbitrary")),
    )(a, w, w, w)


HALO = 16


def _conv_apply_kernel(p_ref, prev_ref, next_ref, b_ref, w_ref, o_ref, *, ctx_rows, total_rows):
    tr = p_ref.shape[0]
    p = p_ref[...].astype(F32)
    r = pl.program_id(0) * tr + lax.broadcasted_iota(jnp.int32, (tr, 1), 0)
    first = lax.broadcasted_iota(jnp.int32, (tr, 1), 0) == 0
    last = lax.broadcasted_iota(jnp.int32, (tr, 1), 0) == tr - 1
    up = jnp.where(first, prev_ref[HALO - 1:HALO, :].astype(F32), pltpu.roll(p, 1, 0))
    dn = jnp.where(last, next_ref[0:1, :].astype(F32), pltpu.roll(p, tr - 1, 0))
    up = jnp.where(jnp.logical_or(r == 0, r == ctx_rows), 0.0, up)
    dn = jnp.where(jnp.logical_or(r == ctx_rows - 1, r == total_rows - 1), 0.0, dn)
    z = w_ref[0:1, :] * up + w_ref[1:2, :] * p + w_ref[2:3, :] * dn
    o_ref[...] = (b_ref[...].astype(F32) * z).astype(BF16)


def _conv_apply(p, b, conv_w, ctx_rows):
    t, d = p.shape
    tr = ROW_TILE
    hb = tr // HALO
    n_halo = t // HALO
    return pl.pallas_call(
        functools.partial(_conv_apply_kernel, ctx_rows=ctx_rows, total_rows=t),
        grid=(t // tr,),
        in_specs=[pl.BlockSpec((tr, d), lambda i: (i, 0)),
                  pl.BlockSpec((HALO, d), lambda i: (jnp.maximum(i * hb - 1, 0), 0)),
                  pl.BlockSpec((HALO, d), lambda i: (jnp.minimum((i + 1) * hb, n_halo - 1), 0)),
                  pl.BlockSpec((tr, d), lambda i: (i, 0)),
                  pl.BlockSpec((8, d), lambda i: (0, 0))],
        out_specs=pl.BlockSpec((tr, d), lambda i: (i, 0)),
        out_shape=jax.ShapeDtypeStruct((t, d), BF16),
        compiler_params=_params(("parallel",)),
    )(p, p, p, b, jnp.pad(conv_w, ((0, 8 - conv_w.shape[0]), (0, 0))))


def _deinterleave_cols(w, n_blocks, width):
    lead = w.shape[:-1]
    return w.reshape(lead + (n_blocks, width // 2, 2)).swapaxes(-1, -2).reshape(lead + (n_blocks * width,))


def _rope_tables(seq, ctx_rows):
    rows = seq // GRID_W
    row = jnp.repeat(jnp.arange(rows, dtype=F32), GRID_W)
    col = jnp.tile(jnp.arange(GRID_W, dtype=F32), rows)

    def cs(rot_dim):
        n_freq = rot_dim // 4
        inv = ROPE_THETA ** (-jnp.arange(n_freq, dtype=F32) / n_freq)
        ang = jnp.concatenate([row[:, None] * inv[None, :], col[:, None] * inv[None, :]], axis=-1)
        half = rot_dim // 2
        cos = jnp.concatenate([jnp.ones((ctx_rows, half), F32), jnp.cos(ang)], axis=0)
        sin = jnp.concatenate([jnp.zeros((ctx_rows, half), F32), jnp.sin(ang)], axis=0)
        return cos, sin

    cos, sin = cs(HEAD_DIM)
    head = (jnp.concatenate([cos, cos], axis=1), jnp.concatenate([-sin, sin], axis=1))
    cos, sin = cs(QK_ROPE)
    z = jnp.zeros_like(cos)
    mla = (jnp.concatenate([cos, cos, z, z], axis=1), jnp.concatenate([z, sin, z, z], axis=1),
           jnp.concatenate([-sin, z, z, z], axis=1))
    return head, mla


def _gqa_weights(w_in, n_heads, n_kv):
    nq, nk = n_heads * HEAD_DIM, n_kv * HEAD_DIM
    return jnp.concatenate([_deinterleave_cols(w_in[:, :nq], n_heads, HEAD_DIM),
                            _deinterleave_cols(w_in[:, nq:nq + nk], n_kv, HEAD_DIM),
                            w_in[:, nq + nk:]], axis=1).astype(BF16)


def kernel(x, c, ctx, c_ctx, ada_w, ada_b, g_mix_pre, g_mix_post, g_ffn_pre, g_ffn_post, win_w_in, win_sink, win_w_out, mla_w_in, mla_g_q, mla_g_kv, mla_w_uq, mla_w_ukv, mla_w_out, qkn_w_in, qkn_g_q, qkn_g_k, qkn_w_out, conv_w_in, conv_w, conv_w_out, ffn_w_in, ffn_w_out, moe_router, moe_w_in, moe_w_out):
    b, s, d = x.shape
    assert b == 1 and c.shape[0] == 1, "one sequence per call"
    l = ctx.shape[1]
    t = l + s
    depth = ada_w.shape[0]
    n_heads = d // HEAD_DIM
    n_kv = (win_w_in.shape[-1] // HEAD_DIM - n_heads) // 2
    rep = n_heads // n_kv
    assert l % ROW_TILE == 0 and s % ROW_TILE == 0 and s % GRID_W == 0

    cc = jnp.zeros((8, d), F32).at[0].set(c[0]).at[1].set(c_ctx)
    mods = _ada(cc, ada_w, ada_b)
    rope_head, rope_mla = _rope_tables(s, l)
    ones_h = jnp.ones((HEAD_DIM,), F32)

    h = jnp.concatenate([ctx[0], x[0]], axis=0)
    u = _modulate(h, g_mix_pre[0], mods[0], 0, l)
    for i in range(depth):
        kind, jdx, f = i % 4, i // 4, i // 2
        last = i == depth - 1
        mod = mods[i]

        if kind == 0:
            qkv = _mm_qkv(u, _gqa_weights(win_w_in[jdx], n_heads, n_kv), *rope_head, ones_h, ones_h,
                          n_heads, n_kv, qk_norm=False)
            o = _flash(qkv, qkv, qkv, win_sink[jdx], groups=n_kv, rep=rep, dq=HEAD_DIM, dv=HEAD_DIM,
                       q_off=0, k_off=n_heads, v_off=n_heads + n_kv, tq=256, tk=256, ctx_rows=l,
                       window=True)
            y = _mm(o, win_w_out[jdx].astype(BF16), F32)
        elif kind == 1:
            q_lora, kv_lora = mla_g_q.shape[-1], mla_g_kv.shape[-1]
            w_in = mla_w_in[jdx]
            w_in = jnp.concatenate(
                [w_in[:, :q_lora + kv_lora], _deinterleave_cols(w_in[:, q_lora + kv_lora:], 1, QK_ROPE),
                 jnp.zeros((d, LANES - QK_ROPE), F32)], axis=1).astype(BF16)
            w_uq = mla_w_uq[jdx].reshape(q_lora, n_heads, QK_NOPE + QK_ROPE)
            w_uq = jnp.concatenate(
                [w_uq[..., :QK_NOPE], _deinterleave_cols(w_uq[..., QK_NOPE:], 1, QK_ROPE),
                 jnp.zeros((q_lora, n_heads, LANES - QK_ROPE), F32)], axis=-1)
            w_uq = w_uq.reshape(q_lora, n_heads * 256).astype(BF16)
            cq, ckv, kr = _mla_in(u, w_in, mla_g_q[jdx], mla_g_kv[jdx], rope_mla)
            q = _mla_q(cq, w_uq, rope_mla)
            k, v = _mla_kv(ckv, mla_w_ukv[jdx].astype(BF16), kr)
            o = _flash(q, k, v, None, groups=n_heads, rep=1, dq=256, dv=V_HEAD, q_off=0, k_off=0,
                       v_off=0, tq=_pick(t, (1280, 640, 256)), tk=_pick(t, (640, 256)), ctx_rows=l,
                       window=False)
            y = _mm(o, mla_w_out[jdx].astype(BF16), F32)
        elif kind == 2:
            qkv = _mm_qkv(u, _gqa_weights(qkn_w_in[jdx], n_heads, n_kv), *rope_head,
                          _deinterleave_cols(qkn_g_q[jdx], 1, HEAD_DIM),
                          _deinterleave_cols(qkn_g_k[jdx], 1, HEAD_DIM), n_heads, n_kv, qk_norm=True)
            o = _flash(qkv, qkv, qkv, None, groups=n_kv, rep=rep, dq=HEAD_DIM, dv=HEAD_DIM, q_off=0,
                       k_off=n_heads, v_off=n_heads + n_kv, tq=256, tk=_pick(t, (1280, 640, 256)),
                       ctx_rows=l, window=False)
            y = _mm(o, qkn_w_out[jdx].astype(BF16), F32)
        else:
            bg, p = _conv_in(u, conv_w_in[jdx].astype(BF16))
            a = _conv_apply(p, bg, conv_w[jdx], l)
            y = _mm(a, conv_w_out[jdx].astype(BF16), F32)

        moe = i % 2 == 1
        res = _resid(h, y, g_mix_post[i], mod, 2, l, nxt=(g_ffn_pre[i], mod, 3),
                     router=moe_router[f] if moe else None)
        h, u = res[0], res[1]

        if moe:
            act = _moe_in(u, moe_w_in[f].astype(BF16), res[2])
            w_out = moe_w_out[f]
            z = _mm(act, w_out.reshape(-1, d).astype(BF16), F32)
        else:
            z = _mm(_mm_swiglu(u, ffn_w_in[f].astype(BF16)), ffn_w_out[f].astype(BF16), F32)

        if last:
            h = _resid(h, z, g_ffn_post[i], mod, 5, l, out_rows=s)[0]
        else:
            h, u = _resid(h, z, g_ffn_post[i], mod, 5, l, nxt=(g_mix_pre[i + 1], mods[i + 1], 0))
    return h.reshape(1, s, d)
```

```python
import functools
import math

import jax
import jax.numpy as jnp
import numpy as np
from jax import lax
from jax.experimental import pallas as pl
from jax.experimental.pallas import tpu as pltpu

HEAD_DIM = 128
GRID_W = 64
QK_NOPE = 128
QK_ROPE = 64
V_HEAD = 128
WINDOW = 128
EPS = 1e-6
ROPE_THETA = 10000.0
NEG_INF = -1e30
LOG2E = 1.4426950408889634
LANES = 128
ROW_TILE = 256
VMEM_LIMIT = 52 * 1024 * 1024

F32 = jnp.float32
BF16 = jnp.bfloat16


def _pick(n, candidates):
    for c in candidates:
        if n % c == 0:
            return c
    raise ValueError(f"no tile for {n} in {candidates}")


def _params(sem):
    return pltpu.CompilerParams(dimension_semantics=sem, vmem_limit_bytes=VMEM_LIMIT)


def _dot(a, b):
    return jnp.dot(a, b, preferred_element_type=F32)


def _silu(x):
    return x / (1.0 + jnp.exp(-x))


def _rms(x, g):
    return x * lax.rsqrt(jnp.mean(x * x, axis=-1, keepdims=True) + EPS) * g


def _ada_kernel(cc_ref, w_ref, b_ref, o_ref):
    a = _silu(cc_ref[...]).astype(BF16)
    o_ref[0] = _dot(a, w_ref[0].astype(BF16)) + b_ref[0]


def _ada(cc, ada_w, ada_b):
    depth, d, n = ada_w.shape
    tn = _pick(n, (1024, 512, 256, 128))
    return pl.pallas_call(
        _ada_kernel,
        grid=(depth, n // tn),
        in_specs=[pl.BlockSpec((8, d), lambda l, j: (0, 0)),
                  pl.BlockSpec((1, d, tn), lambda l, j: (l, 0, j)),
                  pl.BlockSpec((1, 1, tn), lambda l, j: (l, 0, j))],
        out_specs=pl.BlockSpec((1, 8, tn), lambda l, j: (l, 0, j)),
        out_shape=jax.ShapeDtypeStruct((depth, 8, n), F32),
        compiler_params=_params(("parallel", "parallel")),
    )(cc, ada_w, ada_b.reshape(depth, 1, n))


def _mod_vec(mod_ref, is_ctx, k, d):
    return jnp.where(is_ctx, mod_ref[1:2, k * d:(k + 1) * d], mod_ref[0:1, k * d:(k + 1) * d])


def _modulate_kernel(h_ref, g_ref, mod_ref, u_ref, *, ctx_tiles, k0):
    d = h_ref.shape[-1]
    is_ctx = pl.program_id(0) < ctx_tiles
    shift = _mod_vec(mod_ref, is_ctx, k0, d)
    scale = _mod_vec(mod_ref, is_ctx, k0 + 1, d)
    u_ref[...] = (_rms(h_ref[...], g_ref[...]) * (1.0 + scale) + shift).astype(BF16)


def _modulate(h, g, mod, k0, ctx_rows):
    t, d = h.shape
    return pl.pallas_call(
        functools.partial(_modulate_kernel, ctx_tiles=ctx_rows // ROW_TILE, k0=k0),
        grid=(t // ROW_TILE,),
        in_specs=[pl.BlockSpec((ROW_TILE, d), lambda i: (i, 0)),
                  pl.BlockSpec((1, d), lambda i: (0, 0)),
                  pl.BlockSpec((8, 6 * d), lambda i: (0, 0))],
        out_specs=pl.BlockSpec((ROW_TILE, d), lambda i: (i, 0)),
        out_shape=jax.ShapeDtypeStruct((t, d), BF16),
        compiler_params=_params(("parallel",)),
    )(h, g.reshape(1, d), mod)


def _top2_combine(logits, n_experts):
    lane = lax.broadcasted_iota(jnp.int32, logits.shape, 1)
    lg = jnp.where(lane < n_experts, logits, -jnp.inf)
    m1 = jnp.max(lg, axis=1, keepdims=True)
    i1 = jnp.min(jnp.where(lg == m1, lane, LANES), axis=1, keepdims=True)
    lg2 = jnp.where(lane == i1, -jnp.inf, lg)
    m2 = jnp.max(lg2, axis=1, keepdims=True)
    i2 = jnp.min(jnp.where(lg2 == m2, lane, LANES), axis=1, keepdims=True)
    e2 = jnp.exp(m2 - m1)
    g1 = 1.0 / (1.0 + e2)
    g2 = e2 / (1.0 + e2)
    return jnp.where(lane == i1, g1, 0.0) + jnp.where(lane == i2, g2, 0.0)


def _resid_kernel(*refs, ctx_tiles, tile_off, gate_k, next_k, n_experts):
    h_ref, y_ref, gp_ref, mod_ref = refs[:4]
    pos = 4
    if next_k is not None:
        gn_ref, modn_ref = refs[pos:pos + 2]
        pos += 2
    if n_experts:
        router_ref = refs[pos]
        pos += 1
    outs = refs[pos:]
    d = h_ref.shape[-1]
    is_ctx = (pl.program_id(0) + tile_off) < ctx_tiles
    gate = _mod_vec(mod_ref, is_ctx, gate_k, d)
    h = h_ref[...] + gate * _rms(y_ref[...].astype(F32), gp_ref[...])
    outs[0][...] = h
    if next_k is not None:
        shift = _mod_vec(modn_ref, is_ctx, next_k, d)
        scale = _mod_vec(modn_ref, is_ctx, next_k + 1, d)
        u = _rms(h, gn_ref[...]) * (1.0 + scale) + shift
        outs[1][...] = u.astype(BF16)
        if n_experts:
            logits = jnp.dot(u, router_ref[...], preferred_element_type=F32,
                             precision=lax.Precision.HIGHEST)
            outs[2][...] = _top2_combine(logits, n_experts)


def _resid(h, y, g_post, mod, gate_k, ctx_rows, nxt=None, router=None, out_rows=None):
    t, d = h.shape
    off = 0 if out_rows is None else (t - out_rows) // ROW_TILE
    n_out = t if out_rows is None else out_rows
    row = lambda i: (i + off, 0)
    const = lambda i: (0, 0)
    args = [h, y, g_post.reshape(1, d), mod]
    in_specs = [pl.BlockSpec((ROW_TILE, d), row), pl.BlockSpec((ROW_TILE, d), row),
                pl.BlockSpec((1, d), const), pl.BlockSpec((8, 6 * d), const)]
    out_shape = [jax.ShapeDtypeStruct((n_out, d), F32)]
    out_specs = [pl.BlockSpec((ROW_TILE, d), lambda i: (i, 0))]
    n_experts = 0
    if nxt is not None:
        g_pre, mod_next, k0 = nxt
        args += [g_pre.reshape(1, d), mod_next]
        in_specs += [pl.BlockSpec((1, d), const), pl.BlockSpec((8, 6 * d), const)]
        out_shape.append(jax.ShapeDtypeStruct((t, d), BF16))
        out_specs.append(pl.BlockSpec((ROW_TILE, d), lambda i: (i, 0)))
        if router is not None:
            n_experts = router.shape[-1]
            args.append(jnp.pad(router, ((0, 0), (0, LANES - n_experts))))
            in_specs.append(pl.BlockSpec((d, LANES), const))
            out_shape.append(jax.ShapeDtypeStruct((t, LANES), F32))
            out_specs.append(pl.BlockSpec((ROW_TILE, LANES), lambda i: (i, 0)))
    return pl.pallas_call(
        functools.partial(_resid_kernel, ctx_tiles=ctx_rows // ROW_TILE, tile_off=off, gate_k=gate_k,
                          next_k=None if nxt is None else nxt[2], n_experts=n_experts),
        grid=(n_out // ROW_TILE,),
        in_specs=in_specs, out_specs=out_specs, out_shape=out_shape,
        input_output_aliases={0: 0} if out_rows is None else {},
        compiler_params=_params(("parallel",)),
    )(*args)


MM_VMEM_BUDGET = 40 * 1024 * 1024


def _row_tile(t, k, tn, n_w=1, out_bytes=2):
    for tm in (1280, 1024, 640, 512, 256, 128):
        if t % tm:
            continue
        need = 2 * tm * k * 2 + n_w * 2 * k * tn * 2 + 2 * tm * tn * out_bytes + n_w * tm * tn * 4
        if need <= MM_VMEM_BUDGET:
            return tm
    return 0


def _mm_tiles(t, k, n, n_w=1, out_bytes=2, tn_max=512):
    best = (0, 0)
    for tn in (512, 256, 128):
        if tn > tn_max or n % tn:
            continue
        tm = _row_tile(t, k, tn, n_w, out_bytes)
        if tm > best[0]:
            best = (tm, tn)
    assert best[0], f"no matmul tiling for t={t} k={k} n={n}"
    return best


def _mm_kernel(a_ref, w_ref, o_ref):
    o_ref[...] = _dot(a_ref[...], w_ref[...]).astype(o_ref.dtype)


def _mm(a, w, out_dtype):
    t, k = a.shape
    n = w.shape[1]
    tm, tn = _mm_tiles(t, k, n, out_bytes=jnp.dtype(out_dtype).itemsize)
    return pl.pallas_call(
        _mm_kernel,
        grid=(t // tm, n // tn),
        in_specs=[pl.BlockSpec((tm, k), lambda i, j: (i, 0)),
                  pl.BlockSpec((k, tn), lambda i, j: (0, j))],
        out_specs=pl.BlockSpec((tm, tn), lambda i, j: (i, j)),
        out_shape=jax.ShapeDtypeStruct((t, n), out_dtype),
        compiler_params=_params(("parallel", "arbitrary")),
    )(a, w)


def _swiglu_kernel(a_ref, wg_ref, wu_ref, o_ref):
    a = a_ref[...]
    g = _dot(a, wg_ref[...])
    o_ref[...] = (_silu(g) * _dot(a, wu_ref[...])).astype(BF16)


def _mm_swiglu(a, w):
    t, k = a.shape
    f = w.shape[1] // 2
    tm, tn = _mm_tiles(t, k, f, n_w=2)
    nt = f // tn
    return pl.pallas_call(
        _swiglu_kernel,
        grid=(t // tm, nt),
        in_specs=[pl.BlockSpec((tm, k), lambda i, j: (i, 0)),
                  pl.BlockSpec((k, tn), lambda i, j: (0, j)),
                  pl.BlockSpec((k, tn), lambda i, j: (0, j + nt))],
        out_specs=pl.BlockSpec((tm, tn), lambda i, j: (i, j)),
        out_shape=jax.ShapeDtypeStruct((t, f), BF16),
        compiler_params=_params(("parallel", "arbitrary")),
    )(a, w, w)


def _moe_in_kernel(a_ref, wg_ref, wu_ref, comb_ref, o_ref, *, tiles_per_expert):
    e = pl.program_id(1) // tiles_per_expert
    a = a_ref[...]
    g = _dot(a, wg_ref[0])
    up = _dot(a, wu_ref[0])
    comb = comb_ref[...]
    lane = lax.broadcasted_iota(jnp.int32, comb.shape, 1)
    c = jnp.sum(jnp.where(lane == e, comb, 0.0), axis=1, keepdims=True)
    o_ref[...] = (_silu(g) * up * c).astype(BF16)


def _moe_in(a, w_in, comb):
    t, k = a.shape
    n_exp, _, f2 = w_in.shape
    f = f2 // 2
    tm, tn = _mm_tiles(t, k, f, n_w=2)
    nt = f // tn
    return pl.pallas_call(
        functools.partial(_moe_in_kernel, tiles_per_expert=nt),
        grid=(t // tm, n_exp * nt),
        in_specs=[pl.BlockSpec((tm, k), lambda i, j: (i, 0)),
                  pl.BlockSpec((1, k, tn), lambda i, j: (j // nt, 0, j % nt)),
                  pl.BlockSpec((1, k, tn), lambda i, j: (j // nt, 0, j % nt + nt)),
                  pl.BlockSpec((tm, LANES), lambda i, j: (i, 0))],
        out_specs=pl.BlockSpec((tm, tn), lambda i, j: (i, j)),
        out_shape=jax.ShapeDtypeStruct((t, n_exp * f), BF16),
        compiler_params=_params(("parallel", "arbitrary")),
    )(a, w_in, w_in, comb)


def _rope128(x, cos2, sin2):
    return x * cos2 + pltpu.roll(x, 64, 1) * sin2


def _rope64(x, cosp, sa, sb):
    return x * cosp + pltpu.roll(x, 32, 1) * sa + pltpu.roll(x, 96, 1) * sb


def _qkv_kernel(a_ref, w_ref, cos_ref, sin_ref, gq_ref, gk_ref, o_ref, *, q_tiles, k_tiles,
                q_scale, qk_norm):
    j = pl.program_id(1)
    acc = _dot(a_ref[...], w_ref[...])

    @pl.when(j < q_tiles + k_tiles)
    def _():
        is_q = j < q_tiles
        cos2, sin2 = cos_ref[...], sin_ref[...]
        g = jnp.where(is_q, gq_ref[...], gk_ref[...])
        sc = jnp.where(is_q, q_scale, 1.0)
        outs = []
        for hh in range(acc.shape[1] // HEAD_DIM):
            x = acc[:, hh * HEAD_DIM:(hh + 1) * HEAD_DIM]
            if qk_norm:
                x = _rms(x, g)
            outs.append((_rope128(x, cos2, sin2) * sc).astype(BF16))
        o_ref[...] = jnp.concatenate(outs, axis=1)

    @pl.when(j >= q_tiles + k_tiles)
    def _():
        o_ref[...] = acc.astype(BF16)


def _mm_qkv(a, w, cos2, sin2, gq, gk, n_heads, n_kv, qk_norm):
    t, k = a.shape
    n = w.shape[1]
    tm, tn = _mm_tiles(t, k, n_kv * HEAD_DIM)
    const = lambda i, j: (0, 0)
    return pl.pallas_call(
        functools.partial(_qkv_kernel, q_tiles=n_heads * HEAD_DIM // tn, k_tiles=n_kv * HEAD_DIM // tn,
                          q_scale=HEAD_DIM ** -0.5 * LOG2E, qk_norm=qk_norm),
        grid=(t // tm, n // tn),
        in_specs=[pl.BlockSpec((tm, k), lambda i, j: (i, 0)),
                  pl.BlockSpec((k, tn), lambda i, j: (0, j)),
                  pl.BlockSpec((tm, HEAD_DIM), lambda i, j: (i, 0)),
                  pl.BlockSpec((tm, HEAD_DIM), lambda i, j: (i, 0)),
                  pl.BlockSpec((1, HEAD_DIM), const),
                  pl.BlockSpec((1, HEAD_DIM), const)],
        out_specs=pl.BlockSpec((tm, tn), lambda i, j: (i, j)),
        out_shape=jax.ShapeDtypeStruct((t, n), BF16),
        compiler_params=_params(("parallel", "arbitrary")),
    )(a, w, cos2, sin2, gq.reshape(1, HEAD_DIM), gk.reshape(1, HEAD_DIM))


Q_CHUNK = 256
DENSE_KV_CHUNKS = (2048, 1024, 512, 256)


def _flash_kernel(q_ref, k_ref, v_ref, sink_ref, o_ref, m_ref, l_ref, acc_ref, *, rep, dq, dv, tq,
                  tk, ctx_rows, window, use_sink):
    g, i = pl.program_id(0), pl.program_id(1)
    t = k_ref.shape[0]
    n_sub = tq // Q_CHUNK
    chains = [(r, c) for r in range(rep) for c in range(n_sub)]
    row0 = i * tq

    for ch, (r, _) in enumerate(chains):
        if use_sink:
            m_ref[ch] = jnp.full((Q_CHUNK, LANES), sink_ref[g * rep + r] * LOG2E, F32)
            l_ref[ch] = jnp.full((Q_CHUNK, LANES), 1.0 / LANES, F32)
        else:
            m_ref[ch] = jnp.full((Q_CHUNK, LANES), NEG_INF, F32)
            l_ref[ch] = jnp.zeros((Q_CHUNK, LANES), F32)
        acc_ref[ch] = jnp.zeros((Q_CHUNK, dv), F32)

    def scores(ch, k):
        r, c = chains[ch]
        q = q_ref[c * Q_CHUNK:(c + 1) * Q_CHUNK, r * dq:(r + 1) * dq]
        return lax.dot_general(q, k, (((1,), (1,)), ((), ())), preferred_element_type=F32)

    def update(ch, s, v):
        cols = [s[:, j * LANES:(j + 1) * LANES] for j in range(s.shape[1] // LANES)]
        m_prev = m_ref[ch]
        m_new = jnp.maximum(m_prev, jnp.max(functools.reduce(jnp.maximum, cols), axis=1, keepdims=True))
        alpha = jnp.exp2(m_prev - m_new)
        ps = [jnp.exp2(cj - m_new) for cj in cols]
        l_ref[ch] = alpha * l_ref[ch] + functools.reduce(jnp.add, ps)
        p = jnp.concatenate([x.astype(BF16) for x in ps], axis=1)
        acc_ref[ch] = alpha * acc_ref[ch] + _dot(p, v)
        m_ref[ch] = m_new

    def block(k, v, fix):
        s_next = scores(0, k)
        for ch in range(len(chains)):
            s = s_next
            if ch + 1 < len(chains):
                s_next = scores(ch + 1, k)
            update(ch, fix(ch, s), v)

    block(k_ref[0:ctx_rows, :], v_ref[0:ctx_rows, :], lambda ch, s: s)

    if window:
        span = tq + 2 * WINDOW

        @pl.when(row0 >= ctx_rows)
        def _():
            start = pl.multiple_of(jnp.clip(row0 - WINDOW, ctx_rows, t - span), WINDOW)
            qpos = row0 + lax.broadcasted_iota(jnp.int32, (tq, span), 0)
            kpos = start + lax.broadcasted_iota(jnp.int32, (tq, span), 1)
            valid = jnp.abs(kpos - qpos) <= WINDOW
            block(k_ref[pl.ds(start, span), :], v_ref[pl.ds(start, span), :],
                  lambda ch, s: jnp.where(valid, s, NEG_INF))
    else:
        pure = ctx_rows % tq == 0
        n_lat = (t - ctx_rows) // tk

        def fix(ch, s):
            c = chains[ch][1]
            if pure or c * Q_CHUNK >= ctx_rows:
                return s
            return jnp.where(row0 + c * Q_CHUNK < ctx_rows, NEG_INF, s)

        def body(tc, carry):
            start = pl.multiple_of(ctx_rows + tc * tk, Q_CHUNK)
            block(k_ref[pl.ds(start, tk), :], v_ref[pl.ds(start, tk), :], fix)
            return carry

        lax.fori_loop(0, jnp.where(row0 < ctx_rows, 0, n_lat) if pure else n_lat, body, 0)

    for ch, (r, c) in enumerate(chains):
        o = acc_ref[ch] / jnp.sum(l_ref[ch], axis=1, keepdims=True)
        o_ref[c * Q_CHUNK:(c + 1) * Q_CHUNK, r * dv:(r + 1) * dv] = o.astype(BF16)


def _flash(q_arr, k_arr, v_arr, sink, *, groups, rep, dq, dv, q_off, k_off, v_off, tq, tk,
           ctx_rows, window):
    t = q_arr.shape[0]
    assert tq % Q_CHUNK == 0 and t % tq == 0 and ctx_rows % Q_CHUNK == 0
    if window:
        assert tq == Q_CHUNK and ctx_rows % tq == 0 and t - ctx_rows >= tq + 2 * WINDOW
    else:
        assert (t - ctx_rows) % tk == 0 and tk % Q_CHUNK == 0
    use_sink = sink is not None
    if sink is None:
        sink = jnp.zeros((groups * rep,), F32)
    n_chains = rep * (tq // Q_CHUNK)
    return pl.pallas_call(
        functools.partial(_flash_kernel, rep=rep, dq=dq, dv=dv, tq=tq, tk=tk, ctx_rows=ctx_rows,
                          window=window, use_sink=use_sink),
        grid=(groups, t // tq),
        in_specs=[pl.BlockSpec((tq, rep * dq), lambda g, i: (i, q_off // rep + g)),
                  pl.BlockSpec((t, dq), lambda g, i: (0, k_off + g)),
                  pl.BlockSpec((t, dv), lambda g, i: (0, v_off + g)),
                  pl.BlockSpec(memory_space=pltpu.SMEM)],
        out_specs=pl.BlockSpec((tq, rep * dv), lambda g, i: (i, g)),
        out_shape=jax.ShapeDtypeStruct((t, groups * rep * dv), BF16),
        scratch_shapes=[pltpu.VMEM((n_chains, Q_CHUNK, LANES), F32),
                        pltpu.VMEM((n_chains, Q_CHUNK, LANES), F32),
                        pltpu.VMEM((n_chains, Q_CHUNK, dv), F32)],
        compiler_params=_params(("parallel", "arbitrary")),
        name=f"flash_{'window' if window else 'dense'}_dq{dq}",
    )(q_arr, k_arr, v_arr, sink)


def _mla_in_kernel(a_ref, w_ref, gq_ref, gkv_ref, cos_ref, sa_ref, sb_ref, cq_ref, ckv_ref, kr_ref,
                   *, q_lora, kv_lora):
    acc = _dot(a_ref[...], w_ref[...])
    cq_ref[...] = _rms(acc[:, :q_lora], gq_ref[...]).astype(BF16)
    ckv_ref[...] = _rms(acc[:, q_lora:q_lora + kv_lora], gkv_ref[...]).astype(BF16)
    kr = acc[:, q_lora + kv_lora:]
    kr_ref[...] = _rope64(kr, cos_ref[...], sa_ref[...], sb_ref[...]).astype(BF16)


def _mla_in(a, w, g_q, g_kv, tabs):
    t, k = a.shape
    n = w.shape[1]
    q_lora, kv_lora = g_q.shape[0], g_kv.shape[0]
    tm = _pick(t, (256, 128))
    const = lambda i: (0, 0)
    row = lambda i: (i, 0)
    return pl.pallas_call(
        functools.partial(_mla_in_kernel, q_lora=q_lora, kv_lora=kv_lora),
        grid=(t // tm,),
        in_specs=[pl.BlockSpec((tm, k), row), pl.BlockSpec((k, n), const),
                  pl.BlockSpec((1, q_lora), const), pl.BlockSpec((1, kv_lora), const),
                  pl.BlockSpec((tm, LANES), row), pl.BlockSpec((tm, LANES), row),
                  pl.BlockSpec((tm, LANES), row)],
        out_specs=[pl.BlockSpec((tm, q_lora), row), pl.BlockSpec((tm, kv_lora), row),
                   pl.BlockSpec((tm, LANES), row)],
        out_shape=[jax.ShapeDtypeStruct((t, q_lora), BF16), jax.ShapeDtypeStruct((t, kv_lora), BF16),
                   jax.ShapeDtypeStruct((t, LANES), BF16)],
        compiler_params=_params(("parallel",)),
    )(a, w, g_q.reshape(1, -1), g_kv.reshape(1, -1), *tabs)


def _mla_q_kernel(a_ref, w_ref, cos_ref, sa_ref, sb_ref, o_ref, *, q_scale):
    acc = _dot(a_ref[...], w_ref[...])
    cosp, sa, sb = cos_ref[...], sa_ref[...], sb_ref[...]
    outs = []
    for hh in range(acc.shape[1] // 256):
        outs.append((acc[:, hh * 256:hh * 256 + 128] * q_scale).astype(BF16))
        outs.append((_rope64(acc[:, hh * 256 + 128:(hh + 1) * 256], cosp, sa, sb) * q_scale).astype(BF16))
    o_ref[...] = jnp.concatenate(outs, axis=1)


def _mla_q(a, w, tabs):
    t, k = a.shape
    n = w.shape[1]
    tm, tn = _row_tile(t, k, 512), 512
    row = lambda i, j: (i, 0)
    return pl.pallas_call(
        functools.partial(_mla_q_kernel, q_scale=(QK_NOPE + QK_ROPE) ** -0.5 * LOG2E),
        grid=(t // tm, n // tn),
        in_specs=[pl.BlockSpec((tm, k), row), pl.BlockSpec((k, tn), lambda i, j: (0, j)),
                  pl.BlockSpec((tm, LANES), row), pl.BlockSpec((tm, LANES), row),
                  pl.BlockSpec((tm, LANES), row)],
        out_specs=pl.BlockSpec((tm, tn), lambda i, j: (i, j)),
        out_shape=jax.ShapeDtypeStruct((t, n), BF16),
        compiler_params=_params(("parallel", "arbitrary")),
    )(a, w, *tabs)


def _mla_kv_kernel(a_ref, w_ref, kr_ref, k_ref, v_ref):
    acc = _dot(a_ref[...], w_ref[...])
    kr = kr_ref[...]
    ks, vs = [], []
    for hh in range(acc.shape[1] // 256):
        ks += [acc[:, hh * 256:hh * 256 + 128].astype(BF16), kr]
        vs.append(acc[:, hh * 256 + 128:(hh + 1) * 256].astype(BF16))
    k_ref[...] = jnp.concatenate(ks, axis=1)
    v_ref[...] = jnp.concatenate(vs, axis=1) if len(vs) > 1 else vs[0]


def _mla_kv(a, w, kr):
    t, k = a.shape
    n = w.shape[1]
    tm, tn = _row_tile(t, k, 512), 512
    row = lambda i, j: (i, 0)
    return pl.pallas_call(
        _mla_kv_kernel,
        grid=(t // tm, n // tn),
        in_specs=[pl.BlockSpec((tm, k), row), pl.BlockSpec((k, tn), lambda i, j: (0, j)),
                  pl.BlockSpec((tm, LANES), row)],
        out_specs=[pl.BlockSpec((tm, tn), lambda i, j: (i, j)),
                   pl.BlockSpec((tm, tn // 2), lambda i, j: (i, j))],
        out_shape=[jax.ShapeDtypeStruct((t, n), BF16), jax.ShapeDtypeStruct((t, n // 2), BF16)],
        compiler_params=_params(("parallel", "arbitrary")),
    )(a, w, kr)


def _conv_in_kernel(a_ref, wb_ref, wc_ref, wv_ref, b_ref, p_ref):
    a = a_ref[...]
    b_ref[...] = _dot(a, wb_ref[...]).astype(BF16)
    p_ref[...] = (_dot(a, wc_ref[...]) * _dot(a, wv_ref[...])).astype(BF16)


def _conv_in(a, w):
    t, k = a.shape
    d = w.shape[1] // 3
    tm, tn = _mm_tiles(t, k, d, n_w=3, out_bytes=4)
    nt = d // tn
    return pl.pallas_call(
        _conv_in_kernel,
        grid=(t // tm, nt),
        in_specs=[pl.BlockSpec((tm, k), lambda i, j: (i, 0)),
                  pl.BlockSpec((k, tn), lambda i, j: (0, j)),
                  pl.BlockSpec((k, tn), lambda i, j: (0, j + nt)),
                  pl.BlockSpec((k, tn), lambda i, j: (0, j + 2 * nt))],
        out_specs=[pl.BlockSpec((tm, tn), lambda i, j: (i, j))] * 2,
        out_shape=[jax.ShapeDtypeStruct((t, d), BF16)] * 2,
        compiler_params=_params(("parallel", "arbitrary")),
    )(a, w, w, w)


HALO = 16


def _conv_apply_kernel(p_ref, prev_ref, next_ref, b_ref, w_ref, o_ref, *, ctx_rows, total_rows):
    tr = p_ref.shape[0]
    p = p_ref[...].astype(F32)
    r = pl.program_id(0) * tr + lax.broadcasted_iota(jnp.int32, (tr, 1), 0)
    first = lax.broadcasted_iota(jnp.int32, (tr, 1), 0) == 0
    last = lax.broadcasted_iota(jnp.int32, (tr, 1), 0) == tr - 1
    up = jnp.where(first, prev_ref[HALO - 1:HALO, :].astype(F32), pltpu.roll(p, 1, 0))
    dn = jnp.where(last, next_ref[0:1, :].astype(F32), pltpu.roll(p, tr - 1, 0))
    up = jnp.where(jnp.logical_or(r == 0, r == ctx_rows), 0.0, up)
    dn = jnp.where(jnp.logical_or(r == ctx_rows - 1, r == total_rows - 1), 0.0, dn)
    z = w_ref[0:1, :] * up + w_ref[1:2, :] * p + w_ref[2:3, :] * dn
    o_ref[...] = (b_ref[...].astype(F32) * z).astype(BF16)


def _conv_apply(p, b, conv_w, ctx_rows):
    t, d = p.shape
    tr = ROW_TILE
    hb = tr // HALO
    n_halo = t // HALO
    return pl.pallas_call(
        functools.partial(_conv_apply_kernel, ctx_rows=ctx_rows, total_rows=t),
        grid=(t // tr,),
        in_specs=[pl.BlockSpec((tr, d), lambda i: (i, 0)),
                  pl.BlockSpec((HALO, d), lambda i: (jnp.maximum(i * hb - 1, 0), 0)),
                  pl.BlockSpec((HALO, d), lambda i: (jnp.minimum((i + 1) * hb, n_halo - 1), 0)),
                  pl.BlockSpec((tr, d), lambda i: (i, 0)),
                  pl.BlockSpec((8, d), lambda i: (0, 0))],
        out_specs=pl.BlockSpec((tr, d), lambda i: (i, 0)),
        out_shape=jax.ShapeDtypeStruct((t, d), BF16),
        compiler_params=_params(("parallel",)),
    )(p, p, p, b, jnp.pad(conv_w, ((0, 8 - conv_w.shape[0]), (0, 0))))


def _deinterleave_cols(w, n_blocks, width):
    lead = w.shape[:-1]
    return w.reshape(lead + (n_blocks, width // 2, 2)).swapaxes(-1, -2).reshape(lead + (n_blocks * width,))


def _rope_tables(seq, ctx_rows):
    rows = seq // GRID_W
    row = jnp.repeat(jnp.arange(rows, dtype=F32), GRID_W)
    col = jnp.tile(jnp.arange(GRID_W, dtype=F32), rows)

    def cs(rot_dim):
        n_freq = rot_dim // 4
        inv = ROPE_THETA ** (-jnp.arange(n_freq, dtype=F32) / n_freq)
        ang = jnp.concatenate([row[:, None] * inv[None, :], col[:, None] * inv[None, :]], axis=-1)
        half = rot_dim // 2
        cos = jnp.concatenate([jnp.ones((ctx_rows, half), F32), jnp.cos(ang)], axis=0)
        sin = jnp.concatenate([jnp.zeros((ctx_rows, half), F32), jnp.sin(ang)], axis=0)
        return cos, sin

    cos, sin = cs(HEAD_DIM)
    head = (jnp.concatenate([cos, cos], axis=1), jnp.concatenate([-sin, sin], axis=1))
    cos, sin = cs(QK_ROPE)
    z = jnp.zeros_like(cos)
    mla = (jnp.concatenate([cos, cos, z, z], axis=1), jnp.concatenate([z, sin, z, z], axis=1),
           jnp.concatenate([-sin, z, z, z], axis=1))
    return head, mla


def _gqa_weights(w_in, n_heads, n_kv):
    nq, nk = n_heads * HEAD_DIM, n_kv * HEAD_DIM
    return jnp.concatenate([_deinterleave_cols(w_in[:, :nq], n_heads, HEAD_DIM),
                            _deinterleave_cols(w_in[:, nq:nq + nk], n_kv, HEAD_DIM),
                            w_in[:, nq + nk:]], axis=1).astype(BF16)


def kernel(x, c, ctx, c_ctx, ada_w, ada_b, g_mix_pre, g_mix_post, g_ffn_pre, g_ffn_post, win_w_in, win_sink, win_w_out, mla_w_in, mla_g_q, mla_g_kv, mla_w_uq, mla_w_ukv, mla_w_out, qkn_w_in, qkn_g_q, qkn_g_k, qkn_w_out, conv_w_in, conv_w, conv_w_out, ffn_w_in, ffn_w_out, moe_router, moe_w_in, moe_w_out):
    b, s, d = x.shape
    assert b == 1 and c.shape[0] == 1, "one sequence per call"
    l = ctx.shape[1]
    t = l + s
    depth = ada_w.shape[0]
    n_heads = d // HEAD_DIM
    n_kv = (win_w_in.shape[-1] // HEAD_DIM - n_heads) // 2
    rep = n_heads // n_kv
    assert l % ROW_TILE == 0 and s % ROW_TILE == 0 and s % GRID_W == 0

    cc = jnp.zeros((8, d), F32).at[0].set(c[0]).at[1].set(c_ctx)
    mods = _ada(cc, ada_w, ada_b)
    rope_head, rope_mla = _rope_tables(s, l)
    ones_h = jnp.ones((HEAD_DIM,), F32)

    h = jnp.concatenate([ctx[0], x[0]], axis=0)
    u = _modulate(h, g_mix_pre[0], mods[0], 0, l)
    for i in range(depth):
        kind, jdx, f = i % 4, i // 4, i // 2
        last = i == depth - 1
        mod = mods[i]

        if kind == 0:
            qkv = _mm_qkv(u, _gqa_weights(win_w_in[jdx], n_heads, n_kv), *rope_head, ones_h, ones_h,
                          n_heads, n_kv, qk_norm=False)
            o = _flash(qkv, qkv, qkv, win_sink[jdx], groups=n_kv, rep=rep, dq=HEAD_DIM, dv=HEAD_DIM,
                       q_off=0, k_off=n_heads, v_off=n_heads + n_kv, tq=Q_CHUNK, tk=0, ctx_rows=l,
                       window=True)
            y = _mm(o, win_w_out[jdx].astype(BF16), F32)
        elif kind == 1:
            q_lora, kv_lora = mla_g_q.shape[-1], mla_g_kv.shape[-1]
            w_in = mla_w_in[jdx]
            w_in = jnp.concatenate(
                [w_in[:, :q_lora + kv_lora], _deinterleave_cols(w_in[:, q_lora + kv_lora:], 1, QK_ROPE),
                 jnp.zeros((d, LANES - QK_ROPE), F32)], axis=1).astype(BF16)
            w_uq = mla_w_uq[jdx].reshape(q_lora, n_heads, QK_NOPE + QK_ROPE)
            w_uq = jnp.concatenate(
                [w_uq[..., :QK_NOPE], _deinterleave_cols(w_uq[..., QK_NOPE:], 1, QK_ROPE),
                 jnp.zeros((q_lora, n_heads, LANES - QK_ROPE), F32)], axis=-1)
            w_uq = w_uq.reshape(q_lora, n_heads * 256).astype(BF16)
            cq, ckv, kr = _mla_in(u, w_in, mla_g_q[jdx], mla_g_kv[jdx], rope_mla)
            q = _mla_q(cq, w_uq, rope_mla)
            k, v = _mla_kv(ckv, mla_w_ukv[jdx].astype(BF16), kr)
            o = _flash(q, k, v, None, groups=n_heads, rep=1, dq=256, dv=V_HEAD, q_off=0, k_off=0,
                       v_off=0, tq=_pick(t, (1280, 256)), tk=_pick(s, DENSE_KV_CHUNKS), ctx_rows=l,
                       window=False)
            y = _mm(o, mla_w_out[jdx].astype(BF16), F32)
        elif kind == 2:
            qkv = _mm_qkv(u, _gqa_weights(qkn_w_in[jdx], n_heads, n_kv), *rope_head,
                          _deinterleave_cols(qkn_g_q[jdx], 1, HEAD_DIM),
                          _deinterleave_cols(qkn_g_k[jdx], 1, HEAD_DIM), n_heads, n_kv, qk_norm=True)
            o = _flash(qkv, qkv, qkv, None, groups=n_kv, rep=rep, dq=HEAD_DIM, dv=HEAD_DIM, q_off=0,
                       k_off=n_heads, v_off=n_heads + n_kv, tq=Q_CHUNK, tk=_pick(s, DENSE_KV_CHUNKS),
                       ctx_rows=l, window=False)
            y = _mm(o, qkn_w_out[jdx].astype(BF16), F32)
        else:
            bg, p = _conv_in(u, conv_w_in[jdx].astype(BF16))
            a = _conv_apply(p, bg, conv_w[jdx], l)
            y = _mm(a, conv_w_out[jdx].astype(BF16), F32)

        moe = i % 2 == 1
        res = _resid(h, y, g_mix_post[i], mod, 2, l, nxt=(g_ffn_pre[i], mod, 3),
                     router=moe_router[f] if moe else None)
        h, u = res[0], res[1]

        if moe:
            act = _moe_in(u, moe_w_in[f].astype(BF16), res[2])
            w_out = moe_w_out[f]
            z = _mm(act, w_out.reshape(-1, d).astype(BF16), F32)
        else:
            z = _mm(_mm_swiglu(u, ffn_w_in[f].astype(BF16)), ffn_w_out[f].astype(BF16), F32)

        if last:
            h = _resid(h, z, g_ffn_post[i], mod, 5, l, out_rows=s)[0]
        else:
            h, u = _resid(h, z, g_ffn_post[i], mod, 5, l, nxt=(g_mix_pre[i + 1], mods[i + 1], 0))
    return h.reshape(1, s, d)
```

```python
import functools
import math

import jax
import jax.numpy as jnp
import numpy as np
from jax import lax
from jax.experimental import pallas as pl
from jax.experimental.pallas import tpu as pltpu

HEAD_DIM = 128
GRID_W = 64
QK_NOPE = 128
QK_ROPE = 64
V_HEAD = 128
WINDOW = 128
EPS = 1e-6
ROPE_THETA = 10000.0
NEG_INF = -1e30
LOG2E = 1.4426950408889634
LANES = 128
ROW_TILE = 256
VMEM_LIMIT = 52 * 1024 * 1024

F32 = jnp.float32
BF16 = jnp.bfloat16


def _pick(n, candidates):
    for c in candidates:
        if n % c == 0:
            return c
    raise ValueError(f"no tile for {n} in {candidates}")


def _params(sem):
    return pltpu.CompilerParams(dimension_semantics=sem, vmem_limit_bytes=VMEM_LIMIT)


def _dot(a, b):
    return jnp.dot(a, b, preferred_element_type=F32)


def _silu(x):
    return x / (1.0 + jnp.exp(-x))


def _rms(x, g):
    return x * lax.rsqrt(jnp.mean(x * x, axis=-1, keepdims=True) + EPS) * g


def _ada_kernel(cc_ref, w_ref, b_ref, o_ref):
    a = _silu(cc_ref[...]).astype(BF16)
    o_ref[0] = _dot(a, w_ref[0].astype(BF16)) + b_ref[0]


def _ada(cc, ada_w, ada_b):
    depth, d, n = ada_w.shape
    tn = _pick(n, (1024, 512, 256, 128))
    return pl.pallas_call(
        _ada_kernel,
        grid=(depth, n // tn),
        in_specs=[pl.BlockSpec((8, d), lambda l, j: (0, 0)),
                  pl.BlockSpec((1, d, tn), lambda l, j: (l, 0, j)),
                  pl.BlockSpec((1, 1, tn), lambda l, j: (l, 0, j))],
        out_specs=pl.BlockSpec((1, 8, tn), lambda l, j: (l, 0, j)),
        out_shape=jax.ShapeDtypeStruct((depth, 8, n), F32),
        compiler_params=_params(("parallel", "parallel")),
    )(cc, ada_w, ada_b.reshape(depth, 1, n))


def _mod_vec(mod_ref, is_ctx, k, d):
    return jnp.where(is_ctx, mod_ref[1:2, k * d:(k + 1) * d], mod_ref[0:1, k * d:(k + 1) * d])


def _modulate_kernel(h_ref, g_ref, mod_ref, u_ref, *, ctx_tiles, k0):
    d = h_ref.shape[-1]
    is_ctx = pl.program_id(0) < ctx_tiles
    shift = _mod_vec(mod_ref, is_ctx, k0, d)
    scale = _mod_vec(mod_ref, is_ctx, k0 + 1, d)
    u_ref[...] = (_rms(h_ref[...], g_ref[...]) * (1.0 + scale) + shift).astype(BF16)


def _modulate(h, g, mod, k0, ctx_rows):
    t, d = h.shape
    return pl.pallas_call(
        functools.partial(_modulate_kernel, ctx_tiles=ctx_rows // ROW_TILE, k0=k0),
        grid=(t // ROW_TILE,),
        in_specs=[pl.BlockSpec((ROW_TILE, d), lambda i: (i, 0)),
                  pl.BlockSpec((1, d), lambda i: (0, 0)),
                  pl.BlockSpec((8, 6 * d), lambda i: (0, 0))],
        out_specs=pl.BlockSpec((ROW_TILE, d), lambda i: (i, 0)),
        out_shape=jax.ShapeDtypeStruct((t, d), BF16),
        compiler_params=_params(("parallel",)),
    )(h, g.reshape(1, d), mod)


def _pack_bf16_pairs(x):
    half = x.shape[1] // 2
    lo = lax.bitcast_convert_type(x[:, :half].astype(BF16).astype(F32), jnp.uint32)
    hi = lax.bitcast_convert_type(x[:, half:].astype(BF16).astype(F32), jnp.uint32)
    return (hi & jnp.uint32(0xFFFF0000)) | (lo >> 16)


def _unpack_bf16_pairs(w):
    lo = lax.bitcast_convert_type(w << 16, F32)
    hi = lax.bitcast_convert_type(w & jnp.uint32(0xFFFF0000), F32)
    return lo, hi


def _top2_route(logits, n_experts):
    lane = lax.broadcasted_iota(jnp.int32, logits.shape, 1)
    lg = jnp.where(lane < n_experts, logits, -jnp.inf)
    m1 = jnp.max(lg, axis=1, keepdims=True)
    i1 = jnp.min(jnp.where(lg == m1, lane, LANES), axis=1, keepdims=True)
    lg2 = jnp.where(lane == i1, -jnp.inf, lg)
    m2 = jnp.max(lg2, axis=1, keepdims=True)
    i2 = jnp.min(jnp.where(lg2 == m2, lane, LANES), axis=1, keepdims=True)
    e2 = jnp.exp(m2 - m1)
    g1 = 1.0 / (1.0 + e2)
    g2 = e2 / (1.0 + e2)
    out = jnp.where(lane == 0, i1.astype(F32), 0.0) + jnp.where(lane == 1, i2.astype(F32), 0.0)
    return out + jnp.where(lane == 2, g1, 0.0) + jnp.where(lane == 3, g2, 0.0)


def _top2_combine(logits, n_experts):
    lane = lax.broadcasted_iota(jnp.int32, logits.shape, 1)
    lg = jnp.where(lane < n_experts, logits, -jnp.inf)
    m1 = jnp.max(lg, axis=1, keepdims=True)
    i1 = jnp.min(jnp.where(lg == m1, lane, LANES), axis=1, keepdims=True)
    lg2 = jnp.where(lane == i1, -jnp.inf, lg)
    m2 = jnp.max(lg2, axis=1, keepdims=True)
    i2 = jnp.min(jnp.where(lg2 == m2, lane, LANES), axis=1, keepdims=True)
    e2 = jnp.exp(m2 - m1)
    g1 = 1.0 / (1.0 + e2)
    g2 = e2 / (1.0 + e2)
    return jnp.where(lane == i1, g1, 0.0) + jnp.where(lane == i2, g2, 0.0)


def _resid_kernel(*refs, ctx_tiles, tile_off, gate_k, next_k, n_experts, y_packed):
    h_ref, y_ref, gp_ref, mod_ref = refs[:4]
    pos = 4
    if next_k is not None:
        gn_ref, modn_ref = refs[pos:pos + 2]
        pos += 2
    if n_experts:
        router_ref = refs[pos]
        pos += 1
    outs = refs[pos:]
    d = h_ref.shape[-1]
    is_ctx = (pl.program_id(0) + tile_off) < ctx_tiles
    gate = _mod_vec(mod_ref, is_ctx, gate_k, d)
    if y_packed:
        a_lo, a_hi = _unpack_bf16_pairs(y_ref[:, :d // 2])
        b_lo, b_hi = _unpack_bf16_pairs(y_ref[:, d // 2:])
        y = jnp.concatenate([a_lo + b_lo, a_hi + b_hi], axis=1)
    else:
        y = y_ref[...].astype(F32)
    h = h_ref[...] + gate * _rms(y, gp_ref[...])
    outs[0][...] = h
    if next_k is not None:
        shift = _mod_vec(modn_ref, is_ctx, next_k, d)
        scale = _mod_vec(modn_ref, is_ctx, next_k + 1, d)
        u = _rms(h, gn_ref[...]) * (1.0 + scale) + shift
        if n_experts:
            outs[1][...] = _pack_bf16_pairs(u)
            logits = jnp.dot(u, router_ref[...], preferred_element_type=F32,
                             precision=lax.Precision.HIGHEST)
            outs[2][...] = _top2_route(logits, n_experts)
        else:
            outs[1][...] = u.astype(BF16)


def _resid(h, y, g_post, mod, gate_k, ctx_rows, nxt=None, router=None, out_rows=None):
    t, d = h.shape
    off = 0 if out_rows is None else (t - out_rows) // ROW_TILE
    n_out = t if out_rows is None else out_rows
    row = lambda i: (i + off, 0)
    const = lambda i: (0, 0)
    args = [h, y, g_post.reshape(1, d), mod]
    in_specs = [pl.BlockSpec((ROW_TILE, d), row), pl.BlockSpec((ROW_TILE, d), row),
                pl.BlockSpec((1, d), const), pl.BlockSpec((8, 6 * d), const)]
    out_shape = [jax.ShapeDtypeStruct((n_out, d), F32)]
    out_specs = [pl.BlockSpec((ROW_TILE, d), lambda i: (i, 0))]
    n_experts = 0
    if nxt is not None:
        g_pre, mod_next, k0 = nxt
        args += [g_pre.reshape(1, d), mod_next]
        in_specs += [pl.BlockSpec((1, d), const), pl.BlockSpec((8, 6 * d), const)]
        if router is None:
            out_shape.append(jax.ShapeDtypeStruct((t, d), BF16))
            out_specs.append(pl.BlockSpec((ROW_TILE, d), lambda i: (i, 0)))
        else:
            n_experts = router.shape[-1]
            args.append(jnp.pad(router, ((0, 0), (0, LANES - n_experts))))
            in_specs.append(pl.BlockSpec((d, LANES), const))
            out_shape += [jax.ShapeDtypeStruct((t, d // 2), jnp.uint32),
                          jax.ShapeDtypeStruct((t, LANES), F32)]
            out_specs += [pl.BlockSpec((ROW_TILE, d // 2), lambda i: (i, 0)),
                          pl.BlockSpec((ROW_TILE, LANES), lambda i: (i, 0))]
    return pl.pallas_call(
        functools.partial(_resid_kernel, ctx_tiles=ctx_rows // ROW_TILE, tile_off=off, gate_k=gate_k,
                          next_k=None if nxt is None else nxt[2], n_experts=n_experts,
                          y_packed=y.dtype == jnp.uint32),
        grid=(n_out // ROW_TILE,),
        in_specs=in_specs, out_specs=out_specs, out_shape=out_shape,
        input_output_aliases={0: 0} if out_rows is None else {},
        compiler_params=_params(("parallel",)),
    )(*args)


MM_VMEM_BUDGET = 40 * 1024 * 1024


def _row_tile(t, k, tn, n_w=1, out_bytes=2):
    for tm in (1280, 1024, 640, 512, 256, 128):
        if t % tm:
            continue
        need = 2 * tm * k * 2 + n_w * 2 * k * tn * 2 + 2 * tm * tn * out_bytes + n_w * tm * tn * 4
        if need <= MM_VMEM_BUDGET:
            return tm
    return 0


def _mm_tiles(t, k, n, n_w=1, out_bytes=2, tn_max=512):
    best = (0, 0)
    for tn in (512, 256, 128):
        if tn > tn_max or n % tn:
            continue
        tm = _row_tile(t, k, tn, n_w, out_bytes)
        if tm > best[0]:
            best = (tm, tn)
    assert best[0], f"no matmul tiling for t={t} k={k} n={n}"
    return best


def _mm_kernel(a_ref, w_ref, o_ref):
    o_ref[...] = _dot(a_ref[...], w_ref[...]).astype(o_ref.dtype)


def _mm(a, w, out_dtype):
    t, k = a.shape
    n = w.shape[1]
    tm, tn = _mm_tiles(t, k, n, out_bytes=jnp.dtype(out_dtype).itemsize)
    return pl.pallas_call(
        _mm_kernel,
        grid=(t // tm, n // tn),
        in_specs=[pl.BlockSpec((tm, k), lambda i, j: (i, 0)),
                  pl.BlockSpec((k, tn), lambda i, j: (0, j))],
        out_specs=pl.BlockSpec((tm, tn), lambda i, j: (i, j)),
        out_shape=jax.ShapeDtypeStruct((t, n), out_dtype),
        compiler_params=_params(("parallel", "arbitrary")),
    )(a, w)


def _swiglu_kernel(a_ref, wg_ref, wu_ref, o_ref):
    a = a_ref[...]
    g = _dot(a, wg_ref[...])
    o_ref[...] = (_silu(g) * _dot(a, wu_ref[...])).astype(BF16)


def _mm_swiglu(a, w):
    t, k = a.shape
    f = w.shape[1] // 2
    tm, tn = _mm_tiles(t, k, f, n_w=2)
    nt = f // tn
    return pl.pallas_call(
        _swiglu_kernel,
        grid=(t // tm, nt),
        in_specs=[pl.BlockSpec((tm, k), lambda i, j: (i, 0)),
                  pl.BlockSpec((k, tn), lambda i, j: (0, j)),
                  pl.BlockSpec((k, tn), lambda i, j: (0, j + nt))],
        out_specs=pl.BlockSpec((tm, tn), lambda i, j: (i, j)),
        out_shape=jax.ShapeDtypeStruct((t, f), BF16),
        compiler_params=_params(("parallel", "arbitrary")),
    )(a, w, w)


EXPERT_TILE = 256
GATHER_ROWS = 512


def _route_plan(route, n_exp):
    t = route.shape[0]
    n_slots = 2 * t + n_exp * EXPERT_TILE
    n_slots += -n_slots % GATHER_ROWS
    e_flat = route[:, :2].astype(jnp.int32).reshape(-1)
    g_flat = route[:, 2:4].reshape(-1)
    onehot = (e_flat[:, None] == jnp.arange(n_exp, dtype=jnp.int32)[None, :]).astype(jnp.int32)
    csum = jnp.cumsum(onehot, axis=0)
    rank = jnp.sum(csum * onehot, axis=1) - 1
    padded = (csum[-1] + EXPERT_TILE - 1) // EXPERT_TILE * EXPERT_TILE
    g_end = jnp.cumsum(padded)
    slot = jnp.sum(onehot * (g_end - padded)[None, :], axis=1) + rank
    src_token = jnp.zeros((n_slots,), jnp.int32).at[slot].set(jnp.arange(2 * t, dtype=jnp.int32) // 2)
    slot_gate = jnp.zeros((n_slots,), F32).at[slot].set(g_flat)
    tile_start = jnp.arange(n_slots // EXPERT_TILE, dtype=jnp.int32) * EXPERT_TILE
    tile_expert = jnp.minimum(jnp.sum(tile_start[:, None] >= g_end[None, :], axis=1), n_exp - 1)
    n_used = (g_end[-1] // EXPERT_TILE).reshape(1)
    return slot, src_token, slot_gate.reshape(n_slots, 1), tile_expert.astype(jnp.int32), n_used


def _gather_kernel(idx_ref, tab_ref, out_ref, sem):
    base = pl.program_id(0) * GATHER_ROWS

    def row_copy(r):
        return pltpu.make_async_copy(tab_ref.at[pl.ds(idx_ref[0, 0, r], 1)],
                                     out_ref.at[pl.ds(base + r, 1)], sem)

    def issue(r, carry):
        row_copy(r).start()
        return carry

    def drain(r, carry):
        row_copy(r).wait()
        return carry

    lax.fori_loop(0, GATHER_ROWS, issue, 0)
    lax.fori_loop(0, GATHER_ROWS, drain, 0)


def _gather_rows(table, idx):
    n = idx.shape[0]
    assert n % GATHER_ROWS == 0 and table.dtype.itemsize == 4
    return pl.pallas_call(
        _gather_kernel,
        grid=(n // GATHER_ROWS,),
        in_specs=[pl.BlockSpec((1, 1, GATHER_ROWS), lambda i: (i, 0, 0), memory_space=pltpu.SMEM),
                  pl.BlockSpec(memory_space=pl.ANY)],
        out_specs=pl.BlockSpec(memory_space=pl.ANY),
        out_shape=jax.ShapeDtypeStruct((n, table.shape[1]), table.dtype),
        scratch_shapes=[pltpu.SemaphoreType.DMA(())],
        compiler_params=_params(("arbitrary",)),
        name="gather_rows",
    )(idx.reshape(n // GATHER_ROWS, 1, GATHER_ROWS), table)


def _expert_in_kernel(te_ref, nu_ref, x_ref, wg_ref, wu_ref, gate_ref, o_ref):
    @pl.when(pl.program_id(0) < nu_ref[0])
    def _():
        half = wg_ref.shape[1] // 2
        lo, hi = _unpack_bf16_pairs(x_ref[...])
        lo, hi = lo.astype(BF16), hi.astype(BF16)
        g = _dot(lo, wg_ref[0, :half, :]) + _dot(hi, wg_ref[0, half:, :])
        up = _dot(lo, wu_ref[0, :half, :]) + _dot(hi, wu_ref[0, half:, :])
        o_ref[...] = (_silu(g) * up * gate_ref[...]).astype(BF16)

    @pl.when(pl.program_id(0) >= nu_ref[0])
    def _():
        o_ref[...] = jnp.zeros(o_ref.shape, BF16)


def _expert_in(x_slots, w_in, slot_gate, tile_expert, n_used):
    n_slots, half = x_slots.shape
    n_exp, k, f2 = w_in.shape
    f = f2 // 2
    grid_spec = pltpu.PrefetchScalarGridSpec(
        num_scalar_prefetch=2,
        grid=(n_slots // EXPERT_TILE,),
        in_specs=[pl.BlockSpec((EXPERT_TILE, half), lambda m, te, nu: (m, 0)),
                  pl.BlockSpec((1, k, f), lambda m, te, nu: (te[m], 0, 0)),
                  pl.BlockSpec((1, k, f), lambda m, te, nu: (te[m], 0, 1)),
                  pl.BlockSpec((EXPERT_TILE, 1), lambda m, te, nu: (m, 0))],
        out_specs=pl.BlockSpec((EXPERT_TILE, f), lambda m, te, nu: (m, 0)))
    return pl.pallas_call(
        _expert_in_kernel, grid_spec=grid_spec,
        out_shape=jax.ShapeDtypeStruct((n_slots, f), BF16),
        compiler_params=_params(("arbitrary",)),
        name="expert_in",
    )(tile_expert, n_used, x_slots, w_in, w_in, slot_gate)


def _expert_out_kernel(te_ref, nu_ref, a_ref, w_ref, o_ref):
    @pl.when(pl.program_id(0) < nu_ref[0])
    def _():
        o_ref[...] = _pack_bf16_pairs(_dot(a_ref[...], w_ref[0]))

    @pl.when(pl.program_id(0) >= nu_ref[0])
    def _():
        o_ref[...] = jnp.zeros(o_ref.shape, jnp.uint32)


def _expert_out(act, w_out, tile_expert, n_used):
    n_slots, f = act.shape
    d = w_out.shape[2]
    grid_spec = pltpu.PrefetchScalarGridSpec(
        num_scalar_prefetch=2,
        grid=(n_slots // EXPERT_TILE,),
        in_specs=[pl.BlockSpec((EXPERT_TILE, f), lambda m, te, nu: (m, 0)),
                  pl.BlockSpec((1, f, d), lambda m, te, nu: (te[m], 0, 0))],
        out_specs=pl.BlockSpec((EXPERT_TILE, d // 2), lambda m, te, nu: (m, 0)))
    return pl.pallas_call(
        _expert_out_kernel, grid_spec=grid_spec,
        out_shape=jax.ShapeDtypeStruct((n_slots, d // 2), jnp.uint32),
        compiler_params=_params(("arbitrary",)),
        name="expert_out",
    )(tile_expert, n_used, act, w_out)


def _moe(u_packed, route, w_in, w_out):
    t = u_packed.shape[0]
    slot, src_token, slot_gate, tile_expert, n_used = _route_plan(route, w_in.shape[0])
    x_slots = _gather_rows(u_packed, src_token)
    y_slots = _expert_out(_expert_in(x_slots, w_in, slot_gate, tile_expert, n_used), w_out,
                          tile_expert, n_used)
    n_back = 2 * t + (-2 * t) % GATHER_ROWS
    back = jnp.zeros((n_back,), jnp.int32).at[:2 * t].set(slot)
    return _gather_rows(y_slots, back)[:2 * t].reshape(t, -1)


def _rope128(x, cos2, sin2):
    return x * cos2 + pltpu.roll(x, 64, 1) * sin2


def _rope64(x, cosp, sa, sb):
    return x * cosp + pltpu.roll(x, 32, 1) * sa + pltpu.roll(x, 96, 1) * sb


def _qkv_kernel(a_ref, w_ref, cos_ref, sin_ref, gq_ref, gk_ref, o_ref, *, q_tiles, k_tiles,
                q_scale, qk_norm):
    j = pl.program_id(1)
    acc = _dot(a_ref[...], w_ref[...])

    @pl.when(j < q_tiles + k_tiles)
    def _():
        is_q = j < q_tiles
        cos2, sin2 = cos_ref[...], sin_ref[...]
        g = jnp.where(is_q, gq_ref[...], gk_ref[...])
        sc = jnp.where(is_q, q_scale, 1.0)
        outs = []
        for hh in range(acc.shape[1] // HEAD_DIM):
            x = acc[:, hh * HEAD_DIM:(hh + 1) * HEAD_DIM]
            if qk_norm:
                x = _rms(x, g)
            outs.append((_rope128(x, cos2, sin2) * sc).astype(BF16))
        o_ref[...] = jnp.concatenate(outs, axis=1)

    @pl.when(j >= q_tiles + k_tiles)
    def _():
        o_ref[...] = acc.astype(BF16)


def _mm_qkv(a, w, cos2, sin2, gq, gk, n_heads, n_kv, qk_norm):
    t, k = a.shape
    n = w.shape[1]
    tm, tn = _mm_tiles(t, k, n_kv * HEAD_DIM)
    const = lambda i, j: (0, 0)
    return pl.pallas_call(
        functools.partial(_qkv_kernel, q_tiles=n_heads * HEAD_DIM // tn, k_tiles=n_kv * HEAD_DIM // tn,
                          q_scale=HEAD_DIM ** -0.5 * LOG2E, qk_norm=qk_norm),
        grid=(t // tm, n // tn),
        in_specs=[pl.BlockSpec((tm, k), lambda i, j: (i, 0)),
                  pl.BlockSpec((k, tn), lambda i, j: (0, j)),
                  pl.BlockSpec((tm, HEAD_DIM), lambda i, j: (i, 0)),
                  pl.BlockSpec((tm, HEAD_DIM), lambda i, j: (i, 0)),
                  pl.BlockSpec((1, HEAD_DIM), const),
                  pl.BlockSpec((1, HEAD_DIM), const)],
        out_specs=pl.BlockSpec((tm, tn), lambda i, j: (i, j)),
        out_shape=jax.ShapeDtypeStruct((t, n), BF16),
        compiler_params=_params(("parallel", "arbitrary")),
    )(a, w, cos2, sin2, gq.reshape(1, HEAD_DIM), gk.reshape(1, HEAD_DIM))


Q_CHUNK = 256
DENSE_KV_CHUNKS = (2048, 1024, 512, 256)


def _flash_kernel(q_ref, k_ref, v_ref, sink_ref, o_ref, m_ref, l_ref, acc_ref, *, rep, dq, dv, tq,
                  tk, ctx_rows, window, use_sink):
    g, i = pl.program_id(0), pl.program_id(1)
    t = k_ref.shape[0]
    n_sub = tq // Q_CHUNK
    chains = [(r, c) for r in range(rep) for c in range(n_sub)]
    row0 = i * tq

    for ch, (r, _) in enumerate(chains):
        if use_sink:
            m_ref[ch] = jnp.full((Q_CHUNK, LANES), sink_ref[g * rep + r] * LOG2E, F32)
            l_ref[ch] = jnp.full((Q_CHUNK, LANES), 1.0 / LANES, F32)
        else:
            m_ref[ch] = jnp.full((Q_CHUNK, LANES), NEG_INF, F32)
            l_ref[ch] = jnp.zeros((Q_CHUNK, LANES), F32)
        acc_ref[ch] = jnp.zeros((Q_CHUNK, dv), F32)

    def scores(ch, k):
        r, c = chains[ch]
        q = q_ref[c * Q_CHUNK:(c + 1) * Q_CHUNK, r * dq:(r + 1) * dq]
        return lax.dot_general(q, k, (((1,), (1,)), ((), ())), preferred_element_type=F32)

    def update(ch, s, v):
        cols = [s[:, j * LANES:(j + 1) * LANES] for j in range(s.shape[1] // LANES)]
        m_prev = m_ref[ch]
        m_new = jnp.maximum(m_prev, jnp.max(functools.reduce(jnp.maximum, cols), axis=1, keepdims=True))
        alpha = jnp.exp2(m_prev - m_new)
        ps = [jnp.exp2(cj - m_new) for cj in cols]
        l_ref[ch] = alpha * l_ref[ch] + functools.reduce(jnp.add, ps)
        p = jnp.concatenate([x.astype(BF16) for x in ps], axis=1)
        acc_ref[ch] = alpha * acc_ref[ch] + _dot(p, v)
        m_ref[ch] = m_new

    def block(k, v, fix):
        s_next = scores(0, k)
        for ch in range(len(chains)):
            s = s_next
            if ch + 1 < len(chains):
                s_next = scores(ch + 1, k)
            update(ch, fix(ch, s), v)

    block(k_ref[0:ctx_rows, :], v_ref[0:ctx_rows, :], lambda ch, s: s)

    if window:
        span = tq + 2 * WINDOW

        @pl.when(row0 >= ctx_rows)
        def _():
            start = pl.multiple_of(jnp.clip(row0 - WINDOW, ctx_rows, t - span), WINDOW)
            qpos = row0 + lax.broadcasted_iota(jnp.int32, (tq, span), 0)
            kpos = start + lax.broadcasted_iota(jnp.int32, (tq, span), 1)
            valid = jnp.abs(kpos - qpos) <= WINDOW
            block(k_ref[pl.ds(start, span), :], v_ref[pl.ds(start, span), :],
                  lambda ch, s: jnp.where(valid, s, NEG_INF))
    else:
        pure = ctx_rows % tq == 0
        n_lat = (t - ctx_rows) // tk

        def fix(ch, s):
            c = chains[ch][1]
            if pure or c * Q_CHUNK >= ctx_rows:
                return s
            return jnp.where(row0 + c * Q_CHUNK < ctx_rows, NEG_INF, s)

        def body(tc, carry):
            start = pl.multiple_of(ctx_rows + tc * tk, Q_CHUNK)
            block(k_ref[pl.ds(start, tk), :], v_ref[pl.ds(start, tk), :], fix)
            return carry

        skip = jnp.logical_and(pure, row0 < ctx_rows)
        lax.fori_loop(0, jnp.where(skip, 0, n_lat // 2),
                      lambda t2, carry: body(2 * t2 + 1, body(2 * t2, carry)), 0)
        if n_lat % 2:
            @pl.when(jnp.logical_not(skip))
            def _():
                body(n_lat - 1, 0)

    for ch, (r, c) in enumerate(chains):
        o = acc_ref[ch] / jnp.sum(l_ref[ch], axis=1, keepdims=True)
        o_ref[c * Q_CHUNK:(c + 1) * Q_CHUNK, r * dv:(r + 1) * dv] = o.astype(BF16)


def _flash(q_arr, k_arr, v_arr, sink, *, groups, rep, dq, dv, q_off, k_off, v_off, tq, tk,
           ctx_rows, window):
    t = q_arr.shape[0]
    assert tq % Q_CHUNK == 0 and t % tq == 0 and ctx_rows % Q_CHUNK == 0
    if window:
        assert tq == Q_CHUNK and ctx_rows % tq == 0 and t - ctx_rows >= tq + 2 * WINDOW
    else:
        assert (t - ctx_rows) % tk == 0 and tk % Q_CHUNK == 0
    use_sink = sink is not None
    if sink is None:
        sink = jnp.zeros((groups * rep,), F32)
    n_chains = rep * (tq // Q_CHUNK)
    return pl.pallas_call(
        functools.partial(_flash_kernel, rep=rep, dq=dq, dv=dv, tq=tq, tk=tk, ctx_rows=ctx_rows,
                          window=window, use_sink=use_sink),
        grid=(groups, t // tq),
        in_specs=[pl.BlockSpec((tq, rep * dq), lambda g, i: (i, q_off // rep + g)),
                  pl.BlockSpec((t, dq), lambda g, i: (0, k_off + g)),
                  pl.BlockSpec((t, dv), lambda g, i: (0, v_off + g)),
                  pl.BlockSpec(memory_space=pltpu.SMEM)],
        out_specs=pl.BlockSpec((tq, rep * dv), lambda g, i: (i, g)),
        out_shape=jax.ShapeDtypeStruct((t, groups * rep * dv), BF16),
        scratch_shapes=[pltpu.VMEM((n_chains, Q_CHUNK, LANES), F32),
                        pltpu.VMEM((n_chains, Q_CHUNK, LANES), F32),
                        pltpu.VMEM((n_chains, Q_CHUNK, dv), F32)],
        compiler_params=_params(("parallel", "arbitrary")),
        name=f"flash_{'window' if window else 'dense'}_dq{dq}",
    )(q_arr, k_arr, v_arr, sink)


def _mla_in_kernel(a_ref, w_ref, gq_ref, gkv_ref, cos_ref, sa_ref, sb_ref, cq_ref, ckv_ref, kr_ref,
                   *, q_lora, kv_lora):
    acc = _dot(a_ref[...], w_ref[...])
    cq_ref[...] = _rms(acc[:, :q_lora], gq_ref[...]).astype(BF16)
    ckv_ref[...] = _rms(acc[:, q_lora:q_lora + kv_lora], gkv_ref[...]).astype(BF16)
    kr = acc[:, q_lora + kv_lora:]
    kr_ref[...] = _rope64(kr, cos_ref[...], sa_ref[...], sb_ref[...]).astype(BF16)


def _mla_in(a, w, g_q, g_kv, tabs):
    t, k = a.shape
    n = w.shape[1]
    q_lora, kv_lora = g_q.shape[0], g_kv.shape[0]
    tm = _pick(t, (256, 128))
    const = lambda i: (0, 0)
    row = lambda i: (i, 0)
    return pl.pallas_call(
        functools.partial(_mla_in_kernel, q_lora=q_lora, kv_lora=kv_lora),
        grid=(t // tm,),
        in_specs=[pl.BlockSpec((tm, k), row), pl.BlockSpec((k, n), const),
                  pl.BlockSpec((1, q_lora), const), pl.BlockSpec((1, kv_lora), const),
                  pl.BlockSpec((tm, LANES), row), pl.BlockSpec((tm, LANES), row),
                  pl.BlockSpec((tm, LANES), row)],
        out_specs=[pl.BlockSpec((tm, q_lora), row), pl.BlockSpec((tm, kv_lora), row),
                   pl.BlockSpec((tm, LANES), row)],
        out_shape=[jax.ShapeDtypeStruct((t, q_lora), BF16), jax.ShapeDtypeStruct((t, kv_lora), BF16),
                   jax.ShapeDtypeStruct((t, LANES), BF16)],
        compiler_params=_params(("parallel",)),
    )(a, w, g_q.reshape(1, -1), g_kv.reshape(1, -1), *tabs)


def _mla_q_kernel(a_ref, w_ref, cos_ref, sa_ref, sb_ref, o_ref, *, q_scale):
    acc = _dot(a_ref[...], w_ref[...])
    cosp, sa, sb = cos_ref[...], sa_ref[...], sb_ref[...]
    outs = []
    for hh in range(acc.shape[1] // 256):
        outs.append((acc[:, hh * 256:hh * 256 + 128] * q_scale).astype(BF16))
        outs.append((_rope64(acc[:, hh * 256 + 128:(hh + 1) * 256], cosp, sa, sb) * q_scale).astype(BF16))
    o_ref[...] = jnp.concatenate(outs, axis=1)


def _mla_q(a, w, tabs):
    t, k = a.shape
    n = w.shape[1]
    tm, tn = _row_tile(t, k, 512), 512
    row = lambda i, j: (i, 0)
    return pl.pallas_call(
        functools.partial(_mla_q_kernel, q_scale=(QK_NOPE + QK_ROPE) ** -0.5 * LOG2E),
        grid=(t // tm, n // tn),
        in_specs=[pl.BlockSpec((tm, k), row), pl.BlockSpec((k, tn), lambda i, j: (0, j)),
                  pl.BlockSpec((tm, LANES), row), pl.BlockSpec((tm, LANES), row),
                  pl.BlockSpec((tm, LANES), row)],
        out_specs=pl.BlockSpec((tm, tn), lambda i, j: (i, j)),
        out_shape=jax.ShapeDtypeStruct((t, n), BF16),
        compiler_params=_params(("parallel", "arbitrary")),
    )(a, w, *tabs)


def _mla_kv_kernel(a_ref, w_ref, kr_ref, k_ref, v_ref):
    acc = _dot(a_ref[...], w_ref[...])
    kr = kr_ref[...]
    ks, vs = [], []
    for hh in range(acc.shape[1] // 256):
        ks += [acc[:, hh * 256:hh * 256 + 128].astype(BF16), kr]
        vs.append(acc[:, hh * 256 + 128:(hh + 1) * 256].astype(BF16))
    k_ref[...] = jnp.concatenate(ks, axis=1)
    v_ref[...] = jnp.concatenate(vs, axis=1) if len(vs) > 1 else vs[0]


def _mla_kv(a, w, kr):
    t, k = a.shape
    n = w.shape[1]
    tm, tn = _row_tile(t, k, 512), 512
    row = lambda i, j: (i, 0)
    return pl.pallas_call(
        _mla_kv_kernel,
        grid=(t // tm, n // tn),
        in_specs=[pl.BlockSpec((tm, k), row), pl.BlockSpec((k, tn), lambda i, j: (0, j)),
                  pl.BlockSpec((tm, LANES), row)],
        out_specs=[pl.BlockSpec((tm, tn), lambda i, j: (i, j)),
                   pl.BlockSpec((tm, tn // 2), lambda i, j: (i, j))],
        out_shape=[jax.ShapeDtypeStruct((t, n), BF16), jax.ShapeDtypeStruct((t, n // 2), BF16)],
        compiler_params=_params(("parallel", "arbitrary")),
    )(a, w, kr)


def _conv_in_kernel(a_ref, wb_ref, wc_ref, wv_ref, b_ref, p_ref):
    a = a_ref[...]
    b_ref[...] = _dot(a, wb_ref[...]).astype(BF16)
    p_ref[...] = (_dot(a, wc_ref[...]) * _dot(a, wv_ref[...])).astype(BF16)


def _conv_in(a, w):
    t, k = a.shape
    d = w.shape[1] // 3
    tm, tn = _mm_tiles(t, k, d, n_w=3, out_bytes=4)
    nt = d // tn
    return pl.pallas_call(
        _conv_in_kernel,
        grid=(t // tm, nt),
        in_specs=[pl.BlockSpec((tm, k), lambda i, j: (i, 0)),
                  pl.BlockSpec((k, tn), lambda i, j: (0, j)),
                  pl.BlockSpec((k, tn), lambda i, j: (0, j + nt)),
                  pl.BlockSpec((k, tn), lambda i, j: (0, j + 2 * nt))],
        out_specs=[pl.BlockSpec((tm, tn), lambda i, j: (i, j))] * 2,
        out_shape=[jax.ShapeDtypeStruct((t, d), BF16)] * 2,
        compiler_params=_params(("parallel", "arbitrary")),
    )(a, w, w, w)


HALO = 16


def _conv_apply_kernel(p_ref, prev_ref, next_ref, b_ref, w_ref, o_ref, *, ctx_rows, total_rows):
    tr = p_ref.shape[0]
    p = p_ref[...].astype(F32)
    r = pl.program_id(0) * tr + lax.broadcasted_iota(jnp.int32, (tr, 1), 0)
    first = lax.broadcasted_iota(jnp.int32, (tr, 1), 0) == 0
    last = lax.broadcasted_iota(jnp.int32, (tr, 1), 0) == tr - 1
    up = jnp.where(first, prev_ref[HALO - 1:HALO, :].astype(F32), pltpu.roll(p, 1, 0))
    dn = jnp.where(last, next_ref[0:1, :].astype(F32), pltpu.roll(p, tr - 1, 0))
    up = jnp.where(jnp.logical_or(r == 0, r == ctx_rows), 0.0, up)
    dn = jnp.where(jnp.logical_or(r == ctx_rows - 1, r == total_rows - 1), 0.0, dn)
    z = w_ref[0:1, :] * up + w_ref[1:2, :] * p + w_ref[2:3, :] * dn
    o_ref[...] = (b_ref[...].astype(F32) * z).astype(BF16)


def _conv_apply(p, b, conv_w, ctx_rows):
    t, d = p.shape
    tr = ROW_TILE
    hb = tr // HALO
    n_halo = t // HALO
    return pl.pallas_call(
        functools.partial(_conv_apply_kernel, ctx_rows=ctx_rows, total_rows=t),
        grid=(t // tr,),
        in_specs=[pl.BlockSpec((tr, d), lambda i: (i, 0)),
                  pl.BlockSpec((HALO, d), lambda i: (jnp.maximum(i * hb - 1, 0), 0)),
                  pl.BlockSpec((HALO, d), lambda i: (jnp.minimum((i + 1) * hb, n_halo - 1), 0)),
                  pl.BlockSpec((tr, d), lambda i: (i, 0)),
                  pl.BlockSpec((8, d), lambda i: (0, 0))],
        out_specs=pl.BlockSpec((tr, d), lambda i: (i, 0)),
        out_shape=jax.ShapeDtypeStruct((t, d), BF16),
        compiler_params=_params(("parallel",)),
    )(p, p, p, b, jnp.pad(conv_w, ((0, 8 - conv_w.shape[0]), (0, 0))))


def _deinterleave_cols(w, n_blocks, width):
    lead = w.shape[:-1]
    return w.reshape(lead + (n_blocks, width // 2, 2)).swapaxes(-1, -2).reshape(lead + (n_blocks * width,))


def _rope_tables(seq, ctx_rows):
    rows = seq // GRID_W
    row = jnp.repeat(jnp.arange(rows, dtype=F32), GRID_W)
    col = jnp.tile(jnp.arange(GRID_W, dtype=F32), rows)

    def cs(rot_dim):
        n_freq = rot_dim // 4
        inv = ROPE_THETA ** (-jnp.arange(n_freq, dtype=F32) / n_freq)
        ang = jnp.concatenate([row[:, None] * inv[None, :], col[:, None] * inv[None, :]], axis=-1)
        half = rot_dim // 2
        cos = jnp.concatenate([jnp.ones((ctx_rows, half), F32), jnp.cos(ang)], axis=0)
        sin = jnp.concatenate([jnp.zeros((ctx_rows, half), F32), jnp.sin(ang)], axis=0)
        return cos, sin

    cos, sin = cs(HEAD_DIM)
    head = (jnp.concatenate([cos, cos], axis=1), jnp.concatenate([-sin, sin], axis=1))
    cos, sin = cs(QK_ROPE)
    z = jnp.zeros_like(cos)
    mla = (jnp.concatenate([cos, cos, z, z], axis=1), jnp.concatenate([z, sin, z, z], axis=1),
           jnp.concatenate([-sin, z, z, z], axis=1))
    return head, mla


def _gqa_weights(w_in, n_heads, n_kv):
    nq, nk = n_heads * HEAD_DIM, n_kv * HEAD_DIM
    return jnp.concatenate([_deinterleave_cols(w_in[:, :nq], n_heads, HEAD_DIM),
                            _deinterleave_cols(w_in[:, nq:nq + nk], n_kv, HEAD_DIM),
                            w_in[:, nq + nk:]], axis=1).astype(BF16)


def kernel(x, c, ctx, c_ctx, ada_w, ada_b, g_mix_pre, g_mix_post, g_ffn_pre, g_ffn_post, win_w_in, win_sink, win_w_out, mla_w_in, mla_g_q, mla_g_kv, mla_w_uq, mla_w_ukv, mla_w_out, qkn_w_in, qkn_g_q, qkn_g_k, qkn_w_out, conv_w_in, conv_w, conv_w_out, ffn_w_in, ffn_w_out, moe_router, moe_w_in, moe_w_out):
    b, s, d = x.shape
    assert b == 1 and c.shape[0] == 1, "one sequence per call"
    l = ctx.shape[1]
    t = l + s
    depth = ada_w.shape[0]
    n_heads = d // HEAD_DIM
    n_kv = (win_w_in.shape[-1] // HEAD_DIM - n_heads) // 2
    rep = n_heads // n_kv
    assert l % ROW_TILE == 0 and s % ROW_TILE == 0 and s % GRID_W == 0

    cc = jnp.zeros((8, d), F32).at[0].set(c[0]).at[1].set(c_ctx)
    mods = _ada(cc, ada_w, ada_b)
    rope_head, rope_mla = _rope_tables(s, l)
    ones_h = jnp.ones((HEAD_DIM,), F32)

    h = jnp.concatenate([ctx[0], x[0]], axis=0)
    u = _modulate(h, g_mix_pre[0], mods[0], 0, l)
    for i in range(depth):
        kind, jdx, f = i % 4, i // 4, i // 2
        last = i == depth - 1
        mod = mods[i]

        if kind == 0:
            qkv = _mm_qkv(u, _gqa_weights(win_w_in[jdx], n_heads, n_kv), *rope_head, ones_h, ones_h,
                          n_heads, n_kv, qk_norm=False)
            o = _flash(qkv, qkv, qkv, win_sink[jdx], groups=n_kv, rep=rep, dq=HEAD_DIM, dv=HEAD_DIM,
                       q_off=0, k_off=n_heads, v_off=n_heads + n_kv, tq=Q_CHUNK, tk=0, ctx_rows=l,
                       window=True)
            y = _mm(o, win_w_out[jdx].astype(BF16), F32)
        elif kind == 1:
            q_lora, kv_lora = mla_g_q.shape[-1], mla_g_kv.shape[-1]
            w_in = mla_w_in[jdx]
            w_in = jnp.concatenate(
                [w_in[:, :q_lora + kv_lora], _deinterleave_cols(w_in[:, q_lora + kv_lora:], 1, QK_ROPE),
                 jnp.zeros((d, LANES - QK_ROPE), F32)], axis=1).astype(BF16)
            w_uq = mla_w_uq[jdx].reshape(q_lora, n_heads, QK_NOPE + QK_ROPE)
            w_uq = jnp.concatenate(
                [w_uq[..., :QK_NOPE], _deinterleave_cols(w_uq[..., QK_NOPE:], 1, QK_ROPE),
                 jnp.zeros((q_lora, n_heads, LANES - QK_ROPE), F32)], axis=-1)
            w_uq = w_uq.reshape(q_lora, n_heads * 256).astype(BF16)
            cq, ckv, kr = _mla_in(u, w_in, mla_g_q[jdx], mla_g_kv[jdx], rope_mla)
            q = _mla_q(cq, w_uq, rope_mla)
            k, v = _mla_kv(ckv, mla_w_ukv[jdx].astype(BF16), kr)
            o = _flash(q, k, v, None, groups=n_heads, rep=1, dq=256, dv=V_HEAD, q_off=0, k_off=0,
                       v_off=0, tq=_pick(t, (1280, 256)), tk=_pick(s, DENSE_KV_CHUNKS), ctx_rows=l,
                       window=False)
            y = _mm(o, mla_w_out[jdx].astype(BF16), F32)
        elif kind == 2:
            qkv = _mm_qkv(u, _gqa_weights(qkn_w_in[jdx], n_heads, n_kv), *rope_head,
                          _deinterleave_cols(qkn_g_q[jdx], 1, HEAD_DIM),
                          _deinterleave_cols(qkn_g_k[jdx], 1, HEAD_DIM), n_heads, n_kv, qk_norm=True)
            o = _flash(qkv, qkv, qkv, None, groups=n_kv, rep=rep, dq=HEAD_DIM, dv=HEAD_DIM, q_off=0,
                       k_off=n_heads, v_off=n_heads + n_kv, tq=Q_CHUNK, tk=_pick(s, DENSE_KV_CHUNKS),
                       ctx_rows=l, window=False)
            y = _mm(o, qkn_w_out[jdx].astype(BF16), F32)
        else:
            bg, p = _conv_in(u, conv_w_in[jdx].astype(BF16))
            a = _conv_apply(p, bg, conv_w[jdx], l)
            y = _mm(a, conv_w_out[jdx].astype(BF16), F32)

        moe = i % 2 == 1
        res = _resid(h, y, g_mix_post[i], mod, 2, l, nxt=(g_ffn_pre[i], mod, 3),
                     router=moe_router[f] if moe else None)
        h, u = res[0], res[1]

        if moe:
            z = _moe(u, res[2], moe_w_in[f].astype(BF16), moe_w_out[f].astype(BF16))
        else:
            z = _mm(_mm_swiglu(u, ffn_w_in[f].astype(BF16)), ffn_w_out[f].astype(BF16), F32)

        if last:
            h = _resid(h, z, g_ffn_post[i], mod, 5, l, out_rows=s)[0]
        else:
            h, u = _resid(h, z, g_ffn_post[i], mod, 5, l, nxt=(g_mix_pre[i + 1], mods[i + 1], 0))
    return h.reshape(1, s, d)
```

```python
import functools
import math

import jax
import jax.numpy as jnp
import numpy as np
from jax import lax
from jax.experimental import pallas as pl
from jax.experimental.pallas import tpu as pltpu

HEAD_DIM = 128
GRID_W = 64
QK_NOPE = 128
QK_ROPE = 64
V_HEAD = 128
WINDOW = 128
EPS = 1e-6
ROPE_THETA = 10000.0
NEG_INF = -1e30
LOG2E = 1.4426950408889634
LANES = 128
ROW_TILE = 256
VMEM_LIMIT = 52 * 1024 * 1024

F32 = jnp.float32
BF16 = jnp.bfloat16


def _pick(n, candidates):
    for c in candidates:
        if n % c == 0:
            return c
    raise ValueError(f"no tile for {n} in {candidates}")


def _params(sem):
    return pltpu.CompilerParams(dimension_semantics=sem, vmem_limit_bytes=VMEM_LIMIT)


def _dot(a, b):
    return jnp.dot(a, b, preferred_element_type=F32)


def _silu(x):
    return x / (1.0 + jnp.exp(-x))


def _rms(x, g):
    return x * lax.rsqrt(jnp.mean(x * x, axis=-1, keepdims=True) + EPS) * g


def _ada_kernel(cc_ref, w_ref, b_ref, o_ref):
    a = _silu(cc_ref[...]).astype(BF16)
    o_ref[0] = _dot(a, w_ref[0].astype(BF16)) + b_ref[0]


def _ada(cc, ada_w, ada_b):
    depth, d, n = ada_w.shape
    tn = _pick(n, (1024, 512, 256, 128))
    return pl.pallas_call(
        _ada_kernel,
        grid=(depth, n // tn),
        in_specs=[pl.BlockSpec((8, d), lambda l, j: (0, 0)),
                  pl.BlockSpec((1, d, tn), lambda l, j: (l, 0, j)),
                  pl.BlockSpec((1, 1, tn), lambda l, j: (l, 0, j))],
        out_specs=pl.BlockSpec((1, 8, tn), lambda l, j: (l, 0, j)),
        out_shape=jax.ShapeDtypeStruct((depth, 8, n), F32),
        compiler_params=_params(("parallel", "parallel")),
    )(cc, ada_w, ada_b.reshape(depth, 1, n))


def _mod_vec(mod_ref, is_ctx, k, d):
    return jnp.where(is_ctx, mod_ref[1:2, k * d:(k + 1) * d], mod_ref[0:1, k * d:(k + 1) * d])


def _modulate_kernel(h_ref, g_ref, mod_ref, u_ref, *, ctx_tiles, k0):
    d = h_ref.shape[-1]
    is_ctx = pl.program_id(0) < ctx_tiles
    shift = _mod_vec(mod_ref, is_ctx, k0, d)
    scale = _mod_vec(mod_ref, is_ctx, k0 + 1, d)
    u_ref[...] = (_rms(h_ref[...], g_ref[...]) * (1.0 + scale) + shift).astype(BF16)


def _modulate(h, g, mod, k0, ctx_rows):
    t, d = h.shape
    return pl.pallas_call(
        functools.partial(_modulate_kernel, ctx_tiles=ctx_rows // ROW_TILE, k0=k0),
        grid=(t // ROW_TILE,),
        in_specs=[pl.BlockSpec((ROW_TILE, d), lambda i: (i, 0)),
                  pl.BlockSpec((1, d), lambda i: (0, 0)),
                  pl.BlockSpec((8, 6 * d), lambda i: (0, 0))],
        out_specs=pl.BlockSpec((ROW_TILE, d), lambda i: (i, 0)),
        out_shape=jax.ShapeDtypeStruct((t, d), BF16),
        compiler_params=_params(("parallel",)),
    )(h, g.reshape(1, d), mod)


def _pack_bf16_pairs(x):
    half = x.shape[1] // 2
    lo = lax.bitcast_convert_type(x[:, :half].astype(BF16).astype(F32), jnp.uint32)
    hi = lax.bitcast_convert_type(x[:, half:].astype(BF16).astype(F32), jnp.uint32)
    return (hi & jnp.uint32(0xFFFF0000)) | (lo >> 16)


def _unpack_bf16_pairs(w):
    lo = lax.bitcast_convert_type(w << 16, F32)
    hi = lax.bitcast_convert_type(w & jnp.uint32(0xFFFF0000), F32)
    return lo, hi


def _top2_route(logits, n_experts):
    lane = lax.broadcasted_iota(jnp.int32, logits.shape, 1)
    lg = jnp.where(lane < n_experts, logits, -jnp.inf)
    m1 = jnp.max(lg, axis=1, keepdims=True)
    i1 = jnp.min(jnp.where(lg == m1, lane, LANES), axis=1, keepdims=True)
    lg2 = jnp.where(lane == i1, -jnp.inf, lg)
    m2 = jnp.max(lg2, axis=1, keepdims=True)
    i2 = jnp.min(jnp.where(lg2 == m2, lane, LANES), axis=1, keepdims=True)
    e2 = jnp.exp(m2 - m1)
    g1 = 1.0 / (1.0 + e2)
    g2 = e2 / (1.0 + e2)
    out = jnp.where(lane == 0, i1.astype(F32), 0.0) + jnp.where(lane == 1, i2.astype(F32), 0.0)
    return out + jnp.where(lane == 2, g1, 0.0) + jnp.where(lane == 3, g2, 0.0)


def _top2_combine(logits, n_experts):
    lane = lax.broadcasted_iota(jnp.int32, logits.shape, 1)
    lg = jnp.where(lane < n_experts, logits, -jnp.inf)
    m1 = jnp.max(lg, axis=1, keepdims=True)
    i1 = jnp.min(jnp.where(lg == m1, lane, LANES), axis=1, keepdims=True)
    lg2 = jnp.where(lane == i1, -jnp.inf, lg)
    m2 = jnp.max(lg2, axis=1, keepdims=True)
    i2 = jnp.min(jnp.where(lg2 == m2, lane, LANES), axis=1, keepdims=True)
    e2 = jnp.exp(m2 - m1)
    g1 = 1.0 / (1.0 + e2)
    g2 = e2 / (1.0 + e2)
    return jnp.where(lane == i1, g1, 0.0) + jnp.where(lane == i2, g2, 0.0)


def _resid_kernel(*refs, ctx_tiles, tile_off, gate_k, next_k, n_experts, y_packed):
    h_ref, y_ref = refs[:2]
    pos = 2
    if y_packed:
        yb_ref = refs[pos]
        pos += 1
    gp_ref, mod_ref = refs[pos:pos + 2]
    pos += 2
    if next_k is not None:
        gn_ref, modn_ref = refs[pos:pos + 2]
        pos += 2
    if n_experts:
        router_ref = refs[pos]
        pos += 1
    outs = refs[pos:]
    d = h_ref.shape[-1]
    is_ctx = (pl.program_id(0) + tile_off) < ctx_tiles
    gate = _mod_vec(mod_ref, is_ctx, gate_k, d)
    if y_packed:
        a_lo, a_hi = _unpack_bf16_pairs(y_ref[...])
        b_lo, b_hi = _unpack_bf16_pairs(yb_ref[...])
        y = jnp.concatenate([a_lo + b_lo, a_hi + b_hi], axis=1)
    else:
        y = y_ref[...].astype(F32)
    h = h_ref[...] + gate * _rms(y, gp_ref[...])
    outs[0][...] = h
    if next_k is not None:
        shift = _mod_vec(modn_ref, is_ctx, next_k, d)
        scale = _mod_vec(modn_ref, is_ctx, next_k + 1, d)
        u = _rms(h, gn_ref[...]) * (1.0 + scale) + shift
        if n_experts:
            outs[1][...] = _pack_bf16_pairs(u)
            logits = jnp.dot(u, router_ref[...], preferred_element_type=F32,
                             precision=lax.Precision.HIGHEST)
            outs[2][...] = _top2_route(logits, n_experts)
        else:
            outs[1][...] = u.astype(BF16)


def _resid(h, y, g_post, mod, gate_k, ctx_rows, nxt=None, router=None, out_rows=None):
    t, d = h.shape
    off = 0 if out_rows is None else (t - out_rows) // ROW_TILE
    n_out = t if out_rows is None else out_rows
    row = lambda i: (i + off, 0)
    const = lambda i: (0, 0)
    y_packed = y.dtype == jnp.uint32
    if y_packed:
        y_args = [y, y]
        y_specs = [pl.BlockSpec((ROW_TILE, d // 2), row),
                   pl.BlockSpec((ROW_TILE, d // 2), lambda i: (i + off + t // ROW_TILE, 0))]
    else:
        y_args, y_specs = [y], [pl.BlockSpec((ROW_TILE, d), row)]
    args = [h, *y_args, g_post.reshape(1, d), mod]
    in_specs = [pl.BlockSpec((ROW_TILE, d), row), *y_specs,
                pl.BlockSpec((1, d), const), pl.BlockSpec((8, 6 * d), const)]
    out_shape = [jax.ShapeDtypeStruct((n_out, d), F32)]
    out_specs = [pl.BlockSpec((ROW_TILE, d), lambda i: (i, 0))]
    n_experts = 0
    if nxt is not None:
        g_pre, mod_next, k0 = nxt
        args += [g_pre.reshape(1, d), mod_next]
        in_specs += [pl.BlockSpec((1, d), const), pl.BlockSpec((8, 6 * d), const)]
        if router is None:
            out_shape.append(jax.ShapeDtypeStruct((t, d), BF16))
            out_specs.append(pl.BlockSpec((ROW_TILE, d), lambda i: (i, 0)))
        else:
            n_experts = router.shape[-1]
            args.append(jnp.pad(router, ((0, 0), (0, LANES - n_experts))))
            in_specs.append(pl.BlockSpec((d, LANES), const))
            out_shape += [jax.ShapeDtypeStruct((t, d // 2), jnp.uint32),
                          jax.ShapeDtypeStruct((t, LANES), F32)]
            out_specs += [pl.BlockSpec((ROW_TILE, d // 2), lambda i: (i, 0)),
                          pl.BlockSpec((ROW_TILE, LANES), lambda i: (i, 0))]
    return pl.pallas_call(
        functools.partial(_resid_kernel, ctx_tiles=ctx_rows // ROW_TILE, tile_off=off, gate_k=gate_k,
                          next_k=None if nxt is None else nxt[2], n_experts=n_experts,
                          y_packed=y_packed),
        grid=(n_out // ROW_TILE,),
        in_specs=in_specs, out_specs=out_specs, out_shape=out_shape,
        input_output_aliases={0: 0} if out_rows is None else {},
        compiler_params=_params(("parallel",)),
    )(*args)


MM_VMEM_BUDGET = 40 * 1024 * 1024


def _row_tile(t, k, tn, n_w=1, out_bytes=2):
    for tm in (1280, 1024, 640, 512, 256, 128):
        if t % tm:
            continue
        need = 2 * tm * k * 2 + n_w * 2 * k * tn * 2 + 2 * tm * tn * out_bytes + n_w * tm * tn * 4
        if need <= MM_VMEM_BUDGET:
            return tm
    return 0


def _mm_tiles(t, k, n, n_w=1, out_bytes=2, tn_max=512):
    best = (0, 0)
    for tn in (512, 256, 128):
        if tn > tn_max or n % tn:
            continue
        tm = _row_tile(t, k, tn, n_w, out_bytes)
        if tm > best[0]:
            best = (tm, tn)
    assert best[0], f"no matmul tiling for t={t} k={k} n={n}"
    return best


def _mm_kernel(a_ref, w_ref, o_ref):
    o_ref[...] = _dot(a_ref[...], w_ref[...]).astype(o_ref.dtype)


def _mm(a, w, out_dtype):
    t, k = a.shape
    n = w.shape[1]
    tm, tn = _mm_tiles(t, k, n, out_bytes=jnp.dtype(out_dtype).itemsize)
    return pl.pallas_call(
        _mm_kernel,
        grid=(t // tm, n // tn),
        in_specs=[pl.BlockSpec((tm, k), lambda i, j: (i, 0)),
                  pl.BlockSpec((k, tn), lambda i, j: (0, j))],
        out_specs=pl.BlockSpec((tm, tn), lambda i, j: (i, j)),
        out_shape=jax.ShapeDtypeStruct((t, n), out_dtype),
        compiler_params=_params(("parallel", "arbitrary")),
    )(a, w)


def _swiglu_kernel(a_ref, wg_ref, wu_ref, o_ref):
    a = a_ref[...]
    g = _dot(a, wg_ref[...])
    o_ref[...] = (_silu(g) * _dot(a, wu_ref[...])).astype(BF16)


def _mm_swiglu(a, w):
    t, k = a.shape
    f = w.shape[1] // 2
    tm, tn = _mm_tiles(t, k, f, n_w=2)
    nt = f // tn
    return pl.pallas_call(
        _swiglu_kernel,
        grid=(t // tm, nt),
        in_specs=[pl.BlockSpec((tm, k), lambda i, j: (i, 0)),
                  pl.BlockSpec((k, tn), lambda i, j: (0, j)),
                  pl.BlockSpec((k, tn), lambda i, j: (0, j + nt))],
        out_specs=pl.BlockSpec((tm, tn), lambda i, j: (i, j)),
        out_shape=jax.ShapeDtypeStruct((t, f), BF16),
        compiler_params=_params(("parallel", "arbitrary")),
    )(a, w, w)


EXPERT_TILE = 256
GATHER_ROWS = 512


def _route_plan(route, n_exp):
    t = route.shape[0]
    n_slots = 2 * t + n_exp * EXPERT_TILE
    n_slots += -n_slots % GATHER_ROWS
    e_flat = route[:, :2].astype(jnp.int32).reshape(-1)
    g_flat = route[:, 2:4].reshape(-1)
    onehot = (e_flat[:, None] == jnp.arange(n_exp, dtype=jnp.int32)[None, :]).astype(jnp.int32)
    csum = jnp.cumsum(onehot, axis=0)
    rank = jnp.sum(csum * onehot, axis=1) - 1
    padded = (csum[-1] + EXPERT_TILE - 1) // EXPERT_TILE * EXPERT_TILE
    g_end = jnp.cumsum(padded)
    slot = jnp.sum(onehot * (g_end - padded)[None, :], axis=1) + rank
    src_token = jnp.zeros((n_slots,), jnp.int32).at[slot].set(jnp.arange(2 * t, dtype=jnp.int32) // 2)
    slot_gate = jnp.zeros((n_slots,), F32).at[slot].set(g_flat)
    tile_start = jnp.arange(n_slots // EXPERT_TILE, dtype=jnp.int32) * EXPERT_TILE
    tile_expert = jnp.minimum(jnp.sum(tile_start[:, None] >= g_end[None, :], axis=1), n_exp - 1)
    n_used = (g_end[-1] // EXPERT_TILE).reshape(1)
    return slot, src_token, slot_gate.reshape(n_slots, 1), tile_expert.astype(jnp.int32), n_used


def _gather_kernel(idx_ref, tab_ref, out_ref, sem):
    def row_copy(r):
        return pltpu.make_async_copy(tab_ref.at[pl.ds(idx_ref[0, 0, r], 1)],
                                     out_ref.at[pl.ds(r, 1)], sem)

    def issue(r, carry):
        row_copy(r).start()
        return carry

    def drain(r, carry):
        row_copy(r).wait()
        return carry

    lax.fori_loop(0, GATHER_ROWS, issue, 0, unroll=8)
    lax.fori_loop(0, GATHER_ROWS, drain, 0, unroll=8)


def _gather_rows(table, idx):
    n = idx.shape[0]
    assert n % GATHER_ROWS == 0 and table.dtype.itemsize == 4
    return pl.pallas_call(
        _gather_kernel,
        grid=(n // GATHER_ROWS,),
        in_specs=[pl.BlockSpec((1, 1, GATHER_ROWS), lambda i: (i, 0, 0), memory_space=pltpu.SMEM),
                  pl.BlockSpec(memory_space=pl.ANY)],
        out_specs=pl.BlockSpec((GATHER_ROWS, table.shape[1]), lambda i: (i, 0)),
        out_shape=jax.ShapeDtypeStruct((n, table.shape[1]), table.dtype),
        scratch_shapes=[pltpu.SemaphoreType.DMA(())],
        compiler_params=_params(("arbitrary",)),
        name="gather_rows",
    )(idx.reshape(n // GATHER_ROWS, 1, GATHER_ROWS), table)


def _expert_in_kernel(te_ref, nu_ref, x_ref, wg_ref, wu_ref, gate_ref, o_ref):
    @pl.when(pl.program_id(0) < nu_ref[0])
    def _():
        half = wg_ref.shape[1] // 2
        lo, hi = _unpack_bf16_pairs(x_ref[...])
        lo, hi = lo.astype(BF16), hi.astype(BF16)
        g = _dot(lo, wg_ref[0, :half, :]) + _dot(hi, wg_ref[0, half:, :])
        up = _dot(lo, wu_ref[0, :half, :]) + _dot(hi, wu_ref[0, half:, :])
        o_ref[...] = (_silu(g) * up * gate_ref[...]).astype(BF16)

    @pl.when(pl.program_id(0) >= nu_ref[0])
    def _():
        o_ref[...] = jnp.zeros(o_ref.shape, BF16)


def _expert_in(x_slots, w_in, slot_gate, tile_expert, n_used):
    n_slots, half = x_slots.shape
    n_exp, k, f2 = w_in.shape
    f = f2 // 2
    grid_spec = pltpu.PrefetchScalarGridSpec(
        num_scalar_prefetch=2,
        grid=(n_slots // EXPERT_TILE,),
        in_specs=[pl.BlockSpec((EXPERT_TILE, half), lambda m, te, nu: (m, 0)),
                  pl.BlockSpec((1, k, f), lambda m, te, nu: (te[m], 0, 0)),
                  pl.BlockSpec((1, k, f), lambda m, te, nu: (te[m], 0, 1)),
                  pl.BlockSpec((EXPERT_TILE, 1), lambda m, te, nu: (m, 0))],
        out_specs=pl.BlockSpec((EXPERT_TILE, f), lambda m, te, nu: (m, 0)))
    return pl.pallas_call(
        _expert_in_kernel, grid_spec=grid_spec,
        out_shape=jax.ShapeDtypeStruct((n_slots, f), BF16),
        compiler_params=_params(("arbitrary",)),
        name="expert_in",
    )(tile_expert, n_used, x_slots, w_in, w_in, slot_gate)


def _expert_out_kernel(te_ref, nu_ref, a_ref, w_ref, o_ref):
    @pl.when(pl.program_id(0) < nu_ref[0])
    def _():
        o_ref[...] = _pack_bf16_pairs(_dot(a_ref[...], w_ref[0]))

    @pl.when(pl.program_id(0) >= nu_ref[0])
    def _():
        o_ref[...] = jnp.zeros(o_ref.shape, jnp.uint32)


def _expert_out(act, w_out, tile_expert, n_used):
    n_slots, f = act.shape
    d = w_out.shape[2]
    grid_spec = pltpu.PrefetchScalarGridSpec(
        num_scalar_prefetch=2,
        grid=(n_slots // EXPERT_TILE,),
        in_specs=[pl.BlockSpec((EXPERT_TILE, f), lambda m, te, nu: (m, 0)),
                  pl.BlockSpec((1, f, d), lambda m, te, nu: (te[m], 0, 0))],
        out_specs=pl.BlockSpec((EXPERT_TILE, d // 2), lambda m, te, nu: (m, 0)))
    return pl.pallas_call(
        _expert_out_kernel, grid_spec=grid_spec,
        out_shape=jax.ShapeDtypeStruct((n_slots, d // 2), jnp.uint32),
        compiler_params=_params(("arbitrary",)),
        name="expert_out",
    )(tile_expert, n_used, act, w_out)


def _moe(u_packed, route, w_in, w_out):
    t = u_packed.shape[0]
    assert (2 * t) % GATHER_ROWS == 0
    slot, src_token, slot_gate, tile_expert, n_used = _route_plan(route, w_in.shape[0])
    x_slots = _gather_rows(u_packed, src_token)
    y_slots = _expert_out(_expert_in(x_slots, w_in, slot_gate, tile_expert, n_used), w_out,
                          tile_expert, n_used)
    return _gather_rows(y_slots, jnp.concatenate([slot[0::2], slot[1::2]]))


def _rope128(x, cos2, sin2):
    return x * cos2 + pltpu.roll(x, 64, 1) * sin2


def _rope64(x, cosp, sa, sb):
    return x * cosp + pltpu.roll(x, 32, 1) * sa + pltpu.roll(x, 96, 1) * sb


def _qkv_kernel(a_ref, w_ref, cos_ref, sin_ref, gq_ref, gk_ref, o_ref, *, q_tiles, k_tiles,
                q_scale, qk_norm):
    j = pl.program_id(1)
    acc = _dot(a_ref[...], w_ref[...])

    @pl.when(j < q_tiles + k_tiles)
    def _():
        is_q = j < q_tiles
        cos2, sin2 = cos_ref[...], sin_ref[...]
        g = jnp.where(is_q, gq_ref[...], gk_ref[...])
        sc = jnp.where(is_q, q_scale, 1.0)
        outs = []
        for hh in range(acc.shape[1] // HEAD_DIM):
            x = acc[:, hh * HEAD_DIM:(hh + 1) * HEAD_DIM]
            if qk_norm:
                x = _rms(x, g)
            outs.append((_rope128(x, cos2, sin2) * sc).astype(BF16))
        o_ref[...] = jnp.concatenate(outs, axis=1)

    @pl.when(j >= q_tiles + k_tiles)
    def _():
        o_ref[...] = acc.astype(BF16)


def _mm_qkv(a, w, cos2, sin2, gq, gk, n_heads, n_kv, qk_norm):
    t, k = a.shape
    n = w.shape[1]
    tm, tn = _mm_tiles(t, k, n_kv * HEAD_DIM)
    const = lambda i, j: (0, 0)
    return pl.pallas_call(
        functools.partial(_qkv_kernel, q_tiles=n_heads * HEAD_DIM // tn, k_tiles=n_kv * HEAD_DIM // tn,
                          q_scale=HEAD_DIM ** -0.5 * LOG2E, qk_norm=qk_norm),
        grid=(t // tm, n // tn),
        in_specs=[pl.BlockSpec((tm, k), lambda i, j: (i, 0)),
                  pl.BlockSpec((k, tn), lambda i, j: (0, j)),
                  pl.BlockSpec((tm, HEAD_DIM), lambda i, j: (i, 0)),
                  pl.BlockSpec((tm, HEAD_DIM), lambda i, j: (i, 0)),
                  pl.BlockSpec((1, HEAD_DIM), const),
                  pl.BlockSpec((1, HEAD_DIM), const)],
        out_specs=pl.BlockSpec((tm, tn), lambda i, j: (i, j)),
        out_shape=jax.ShapeDtypeStruct((t, n), BF16),
        compiler_params=_params(("parallel", "arbitrary")),
    )(a, w, cos2, sin2, gq.reshape(1, HEAD_DIM), gk.reshape(1, HEAD_DIM))


Q_CHUNK = 256
DENSE_KV_CHUNKS = (2048, 1024, 512, 256)


def _flash_kernel(q_ref, k_ref, v_ref, sink_ref, o_ref, m_ref, l_ref, acc_ref, *, rep, dq, dv, tq,
                  tk, ctx_rows, window, use_sink):
    g, i = pl.program_id(0), pl.program_id(1)
    t = k_ref.shape[0]
    n_sub = tq // Q_CHUNK
    chains = [(r, c) for r in range(rep) for c in range(n_sub)]
    row0 = i * tq

    for ch, (r, _) in enumerate(chains):
        if use_sink:
            m_ref[ch] = jnp.full((Q_CHUNK, LANES), sink_ref[g * rep + r] * LOG2E, F32)
            l_ref[ch] = jnp.full((Q_CHUNK, LANES), 1.0 / LANES, F32)
        else:
            m_ref[ch] = jnp.full((Q_CHUNK, LANES), NEG_INF, F32)
            l_ref[ch] = jnp.zeros((Q_CHUNK, LANES), F32)
        acc_ref[ch] = jnp.zeros((Q_CHUNK, dv), F32)

    def scores(ch, k):
        r, c = chains[ch]
        q = q_ref[c * Q_CHUNK:(c + 1) * Q_CHUNK, r * dq:(r + 1) * dq]
        return lax.dot_general(q, k, (((1,), (1,)), ((), ())), preferred_element_type=F32)

    def update(ch, s, v):
        cols = [s[:, j * LANES:(j + 1) * LANES] for j in range(s.shape[1] // LANES)]
        m_prev = m_ref[ch]
        m_new = jnp.maximum(m_prev, jnp.max(functools.reduce(jnp.maximum, cols), axis=1, keepdims=True))
        alpha = jnp.exp2(m_prev - m_new)
        ps = [jnp.exp2(cj - m_new) for cj in cols]
        l_ref[ch] = alpha * l_ref[ch] + functools.reduce(jnp.add, ps)
        p = jnp.concatenate([x.astype(BF16) for x in ps], axis=1)
        acc_ref[ch] = alpha * acc_ref[ch] + _dot(p, v)
        m_ref[ch] = m_new

    def block(k, v, fix):
        s_next = scores(0, k)
        for ch in range(len(chains)):
            s = s_next
            if ch + 1 < len(chains):
                s_next = scores(ch + 1, k)
            update(ch, fix(ch, s), v)

    block(k_ref[0:ctx_rows, :], v_ref[0:ctx_rows, :], lambda ch, s: s)

    if window:
        span = tq + 2 * WINDOW

        @pl.when(row0 >= ctx_rows)
        def _():
            start = pl.multiple_of(jnp.clip(row0 - WINDOW, ctx_rows, t - span), WINDOW)
            qpos = row0 + lax.broadcasted_iota(jnp.int32, (tq, span), 0)
            kpos = start + lax.broadcasted_iota(jnp.int32, (tq, span), 1)
            valid = jnp.abs(kpos - qpos) <= WINDOW
            block(k_ref[pl.ds(start, span), :], v_ref[pl.ds(start, span), :],
                  lambda ch, s: jnp.where(valid, s, NEG_INF))
    else:
        pure = ctx_rows % tq == 0
        n_lat = (t - ctx_rows) // tk

        def fix(ch, s):
            c = chains[ch][1]
            if pure or c * Q_CHUNK >= ctx_rows:
                return s
            return jnp.where(row0 + c * Q_CHUNK < ctx_rows, NEG_INF, s)

        def body(tc, carry):
            start = pl.multiple_of(ctx_rows + tc * tk, Q_CHUNK)
            block(k_ref[pl.ds(start, tk), :], v_ref[pl.ds(start, tk), :], fix)
            return carry

        skip = jnp.logical_and(pure, row0 < ctx_rows)
        lax.fori_loop(0, jnp.where(skip, 0, n_lat // 2),
                      lambda t2, carry: body(2 * t2 + 1, body(2 * t2, carry)), 0)
        if n_lat % 2:
            @pl.when(jnp.logical_not(skip))
            def _():
                body(n_lat - 1, 0)

    for ch, (r, c) in enumerate(chains):
        o = acc_ref[ch] / jnp.sum(l_ref[ch], axis=1, keepdims=True)
        o_ref[c * Q_CHUNK:(c + 1) * Q_CHUNK, r * dv:(r + 1) * dv] = o.astype(BF16)


def _flash(q_arr, k_arr, v_arr, sink, *, groups, rep, dq, dv, q_off, k_off, v_off, tq, tk,
           ctx_rows, window):
    t = q_arr.shape[0]
    assert tq % Q_CHUNK == 0 and t % tq == 0 and ctx_rows % Q_CHUNK == 0
    if window:
        assert tq == Q_CHUNK and ctx_rows % tq == 0 and t - ctx_rows >= tq + 2 * WINDOW
    else:
        assert (t - ctx_rows) % tk == 0 and tk % Q_CHUNK == 0
    use_sink = sink is not None
    if sink is None:
        sink = jnp.zeros((groups * rep,), F32)
    n_chains = rep * (tq // Q_CHUNK)
    return pl.pallas_call(
        functools.partial(_flash_kernel, rep=rep, dq=dq, dv=dv, tq=tq, tk=tk, ctx_rows=ctx_rows,
                          window=window, use_sink=use_sink),
        grid=(groups, t // tq),
        in_specs=[pl.BlockSpec((tq, rep * dq), lambda g, i: (i, q_off // rep + g)),
                  pl.BlockSpec((t, dq), lambda g, i: (0, k_off + g)),
                  pl.BlockSpec((t, dv), lambda g, i: (0, v_off + g)),
                  pl.BlockSpec(memory_space=pltpu.SMEM)],
        out_specs=pl.BlockSpec((tq, rep * dv), lambda g, i: (i, g)),
        out_shape=jax.ShapeDtypeStruct((t, groups * rep * dv), BF16),
        scratch_shapes=[pltpu.VMEM((n_chains, Q_CHUNK, LANES), F32),
                        pltpu.VMEM((n_chains, Q_CHUNK, LANES), F32),
                        pltpu.VMEM((n_chains, Q_CHUNK, dv), F32)],
        compiler_params=_params(("parallel", "arbitrary")),
        name=f"flash_{'window' if window else 'dense'}_dq{dq}",
    )(q_arr, k_arr, v_arr, sink)


def _mla_in_kernel(a_ref, w_ref, gq_ref, gkv_ref, cos_ref, sa_ref, sb_ref, cq_ref, ckv_ref, kr_ref,
                   *, q_lora, kv_lora):
    acc = _dot(a_ref[...], w_ref[...])
    cq_ref[...] = _rms(acc[:, :q_lora], gq_ref[...]).astype(BF16)
    ckv_ref[...] = _rms(acc[:, q_lora:q_lora + kv_lora], gkv_ref[...]).astype(BF16)
    kr = acc[:, q_lora + kv_lora:]
    kr_ref[...] = _rope64(kr, cos_ref[...], sa_ref[...], sb_ref[...]).astype(BF16)


def _mla_in(a, w, g_q, g_kv, tabs):
    t, k = a.shape
    n = w.shape[1]
    q_lora, kv_lora = g_q.shape[0], g_kv.shape[0]
    tm = _pick(t, (256, 128))
    const = lambda i: (0, 0)
    row = lambda i: (i, 0)
    return pl.pallas_call(
        functools.partial(_mla_in_kernel, q_lora=q_lora, kv_lora=kv_lora),
        grid=(t // tm,),
        in_specs=[pl.BlockSpec((tm, k), row), pl.BlockSpec((k, n), const),
                  pl.BlockSpec((1, q_lora), const), pl.BlockSpec((1, kv_lora), const),
                  pl.BlockSpec((tm, LANES), row), pl.BlockSpec((tm, LANES), row),
                  pl.BlockSpec((tm, LANES), row)],
        out_specs=[pl.BlockSpec((tm, q_lora), row), pl.BlockSpec((tm, kv_lora), row),
                   pl.BlockSpec((tm, LANES), row)],
        out_shape=[jax.ShapeDtypeStruct((t, q_lora), BF16), jax.ShapeDtypeStruct((t, kv_lora), BF16),
                   jax.ShapeDtypeStruct((t, LANES), BF16)],
        compiler_params=_params(("parallel",)),
    )(a, w, g_q.reshape(1, -1), g_kv.reshape(1, -1), *tabs)


def _mla_q_kernel(a_ref, w_ref, cos_ref, sa_ref, sb_ref, o_ref, *, q_scale):
    acc = _dot(a_ref[...], w_ref[...])
    cosp, sa, sb = cos_ref[...], sa_ref[...], sb_ref[...]
    outs = []
    for hh in range(acc.shape[1] // 256):
        outs.append((acc[:, hh * 256:hh * 256 + 128] * q_scale).astype(BF16))
        outs.append((_rope64(acc[:, hh * 256 + 128:(hh + 1) * 256], cosp, sa, sb) * q_scale).astype(BF16))
    o_ref[...] = jnp.concatenate(outs, axis=1)


def _mla_q(a, w, tabs):
    t, k = a.shape
    n = w.shape[1]
    tm, tn = _row_tile(t, k, 512), 512
    row = lambda i, j: (i, 0)
    return pl.pallas_call(
        functools.partial(_mla_q_kernel, q_scale=(QK_NOPE + QK_ROPE) ** -0.5 * LOG2E),
        grid=(t // tm, n // tn),
        in_specs=[pl.BlockSpec((tm, k), row), pl.BlockSpec((k, tn), lambda i, j: (0, j)),
                  pl.BlockSpec((tm, LANES), row), pl.BlockSpec((tm, LANES), row),
                  pl.BlockSpec((tm, LANES), row)],
        out_specs=pl.BlockSpec((tm, tn), lambda i, j: (i, j)),
        out_shape=jax.ShapeDtypeStruct((t, n), BF16),
        compiler_params=_params(("parallel", "arbitrary")),
    )(a, w, *tabs)


def _mla_kv_kernel(a_ref, w_ref, kr_ref, k_ref, v_ref):
    acc = _dot(a_ref[...], w_ref[...])
    kr = kr_ref[...]
    ks, vs = [], []
    for hh in range(acc.shape[1] // 256):
        ks += [acc[:, hh * 256:hh * 256 + 128].astype(BF16), kr]
        vs.append(acc[:, hh * 256 + 128:(hh + 1) * 256].astype(BF16))
    k_ref[...] = jnp.concatenate(ks, axis=1)
    v_ref[...] = jnp.concatenate(vs, axis=1) if len(vs) > 1 else vs[0]


def _mla_kv(a, w, kr):
    t, k = a.shape
    n = w.shape[1]
    tm, tn = _row_tile(t, k, 512), 512
    row = lambda i, j: (i, 0)
    return pl.pallas_call(
        _mla_kv_kernel,
        grid=(t // tm, n // tn),
        in_specs=[pl.BlockSpec((tm, k), row), pl.BlockSpec((k, tn), lambda i, j: (0, j)),
                  pl.BlockSpec((tm, LANES), row)],
        out_specs=[pl.BlockSpec((tm, tn), lambda i, j: (i, j)),
                   pl.BlockSpec((tm, tn // 2), lambda i, j: (i, j))],
        out_shape=[jax.ShapeDtypeStruct((t, n), BF16), jax.ShapeDtypeStruct((t, n // 2), BF16)],
        compiler_params=_params(("parallel", "arbitrary")),
    )(a, w, kr)


def _conv_in_kernel(a_ref, wb_ref, wc_ref, wv_ref, b_ref, p_ref):
    a = a_ref[...]
    b_ref[...] = _dot(a, wb_ref[...]).astype(BF16)
    p_ref[...] = (_dot(a, wc_ref[...]) * _dot(a, wv_ref[...])).astype(BF16)


def _conv_in(a, w):
    t, k = a.shape
    d = w.shape[1] // 3
    tm, tn = _mm_tiles(t, k, d, n_w=3, out_bytes=4)
    nt = d // tn
    return pl.pallas_call(
        _conv_in_kernel,
        grid=(t // tm, nt),
        in_specs=[pl.BlockSpec((tm, k), lambda i, j: (i, 0)),
                  pl.BlockSpec((k, tn), lambda i, j: (0, j)),
                  pl.BlockSpec((k, tn), lambda i, j: (0, j + nt)),
                  pl.BlockSpec((k, tn), lambda i, j: (0, j + 2 * nt))],
        out_specs=[pl.BlockSpec((tm, tn), lambda i, j: (i, j))] * 2,
        out_shape=[jax.ShapeDtypeStruct((t, d), BF16)] * 2,
        compiler_params=_params(("parallel", "arbitrary")),
    )(a, w, w, w)


HALO = 16


def _conv_apply_kernel(p_ref, prev_ref, next_ref, b_ref, w_ref, o_ref, *, ctx_rows, total_rows):
    tr = p_ref.shape[0]
    p = p_ref[...].astype(F32)
    r = pl.program_id(0) * tr + lax.broadcasted_iota(jnp.int32, (tr, 1), 0)
    first = lax.broadcasted_iota(jnp.int32, (tr, 1), 0) == 0
    last = lax.broadcasted_iota(jnp.int32, (tr, 1), 0) == tr - 1
    up = jnp.where(first, prev_ref[HALO - 1:HALO, :].astype(F32), pltpu.roll(p, 1, 0))
    dn = jnp.where(last, next_ref[0:1, :].astype(F32), pltpu.roll(p, tr - 1, 0))
    up = jnp.where(jnp.logical_or(r == 0, r == ctx_rows), 0.0, up)
    dn = jnp.where(jnp.logical_or(r == ctx_rows - 1, r == total_rows - 1), 0.0, dn)
    z = w_ref[0:1, :] * up + w_ref[1:2, :] * p + w_ref[2:3, :] * dn
    o_ref[...] = (b_ref[...].astype(F32) * z).astype(BF16)


def _conv_apply(p, b, conv_w, ctx_rows):
    t, d = p.shape
    tr = ROW_TILE
    hb = tr // HALO
    n_halo = t // HALO
    return pl.pallas_call(
        functools.partial(_conv_apply_kernel, ctx_rows=ctx_rows, total_rows=t),
        grid=(t // tr,),
        in_specs=[pl.BlockSpec((tr, d), lambda i: (i, 0)),
                  pl.BlockSpec((HALO, d), lambda i: (jnp.maximum(i * hb - 1, 0), 0)),
                  pl.BlockSpec((HALO, d), lambda i: (jnp.minimum((i + 1) * hb, n_halo - 1), 0)),
                  pl.BlockSpec((tr, d), lambda i: (i, 0)),
                  pl.BlockSpec((8, d), lambda i: (0, 0))],
        out_specs=pl.BlockSpec((tr, d), lambda i: (i, 0)),
        out_shape=jax.ShapeDtypeStruct((t, d), BF16),
        compiler_params=_params(("parallel",)),
    )(p, p, p, b, jnp.pad(conv_w, ((0, 8 - conv_w.shape[0]), (0, 0))))


def _deinterleave_cols(w, n_blocks, width):
    lead = w.shape[:-1]
    return w.reshape(lead + (n_blocks, width // 2, 2)).swapaxes(-1, -2).reshape(lead + (n_blocks * width,))


def _rope_tables(seq, ctx_rows):
    rows = seq // GRID_W
    row = jnp.repeat(jnp.arange(rows, dtype=F32), GRID_W)
    col = jnp.tile(jnp.arange(GRID_W, dtype=F32), rows)

    def cs(rot_dim):
        n_freq = rot_dim // 4
        inv = ROPE_THETA ** (-jnp.arange(n_freq, dtype=F32) / n_freq)
        ang = jnp.concatenate([row[:, None] * inv[None, :], col[:, None] * inv[None, :]], axis=-1)
        half = rot_dim // 2
        cos = jnp.concatenate([jnp.ones((ctx_rows, half), F32), jnp.cos(ang)], axis=0)
        sin = jnp.concatenate([jnp.zeros((ctx_rows, half), F32), jnp.sin(ang)], axis=0)
        return cos, sin

    cos, sin = cs(HEAD_DIM)
    head = (jnp.concatenate([cos, cos], axis=1), jnp.concatenate([-sin, sin], axis=1))
    cos, sin = cs(QK_ROPE)
    z = jnp.zeros_like(cos)
    mla = (jnp.concatenate([cos, cos, z, z], axis=1), jnp.concatenate([z, sin, z, z], axis=1),
           jnp.concatenate([-sin, z, z, z], axis=1))
    return head, mla


def _gqa_weights(w_in, n_heads, n_kv):
    nq, nk = n_heads * HEAD_DIM, n_kv * HEAD_DIM
    return jnp.concatenate([_deinterleave_cols(w_in[:, :nq], n_heads, HEAD_DIM),
                            _deinterleave_cols(w_in[:, nq:nq + nk], n_kv, HEAD_DIM),
                            w_in[:, nq + nk:]], axis=1).astype(BF16)


def kernel(x, c, ctx, c_ctx, ada_w, ada_b, g_mix_pre, g_mix_post, g_ffn_pre, g_ffn_post, win_w_in, win_sink, win_w_out, mla_w_in, mla_g_q, mla_g_kv, mla_w_uq, mla_w_ukv, mla_w_out, qkn_w_in, qkn_g_q, qkn_g_k, qkn_w_out, conv_w_in, conv_w, conv_w_out, ffn_w_in, ffn_w_out, moe_router, moe_w_in, moe_w_out):
    b, s, d = x.shape
    assert b == 1 and c.shape[0] == 1, "one sequence per call"
    l = ctx.shape[1]
    t = l + s
    depth = ada_w.shape[0]
    n_heads = d // HEAD_DIM
    n_kv = (win_w_in.shape[-1] // HEAD_DIM - n_heads) // 2
    rep = n_heads // n_kv
    assert l % ROW_TILE == 0 and s % ROW_TILE == 0 and s % GRID_W == 0

    cc = jnp.zeros((8, d), F32).at[0].set(c[0]).at[1].set(c_ctx)
    mods = _ada(cc, ada_w, ada_b)
    rope_head, rope_mla = _rope_tables(s, l)
    ones_h = jnp.ones((HEAD_DIM,), F32)

    h = jnp.concatenate([ctx[0], x[0]], axis=0)
    u = _modulate(h, g_mix_pre[0], mods[0], 0, l)
    for i in range(depth):
        kind, jdx, f = i % 4, i // 4, i // 2
        last = i == depth - 1
        mod = mods[i]

        if kind == 0:
            qkv = _mm_qkv(u, _gqa_weights(win_w_in[jdx], n_heads, n_kv), *rope_head, ones_h, ones_h,
                          n_heads, n_kv, qk_norm=False)
            o = _flash(qkv, qkv, qkv, win_sink[jdx], groups=n_kv, rep=rep, dq=HEAD_DIM, dv=HEAD_DIM,
                       q_off=0, k_off=n_heads, v_off=n_heads + n_kv, tq=Q_CHUNK, tk=0, ctx_rows=l,
                       window=True)
            y = _mm(o, win_w_out[jdx].astype(BF16), F32)
        elif kind == 1:
            q_lora, kv_lora = mla_g_q.shape[-1], mla_g_kv.shape[-1]
            w_in = mla_w_in[jdx]
            w_in = jnp.concatenate(
                [w_in[:, :q_lora + kv_lora], _deinterleave_cols(w_in[:, q_lora + kv_lora:], 1, QK_ROPE),
                 jnp.zeros((d, LANES - QK_ROPE), F32)], axis=1).astype(BF16)
            w_uq = mla_w_uq[jdx].reshape(q_lora, n_heads, QK_NOPE + QK_ROPE)
            w_uq = jnp.concatenate(
                [w_uq[..., :QK_NOPE], _deinterleave_cols(w_uq[..., QK_NOPE:], 1, QK_ROPE),
                 jnp.zeros((q_lora, n_heads, LANES - QK_ROPE), F32)], axis=-1)
            w_uq = w_uq.reshape(q_lora, n_heads * 256).astype(BF16)
            cq, ckv, kr = _mla_in(u, w_in, mla_g_q[jdx], mla_g_kv[jdx], rope_mla)
            q = _mla_q(cq, w_uq, rope_mla)
            k, v = _mla_kv(ckv, mla_w_ukv[jdx].astype(BF16), kr)
            o = _flash(q, k, v, None, groups=n_heads, rep=1, dq=256, dv=V_HEAD, q_off=0, k_off=0,
                       v_off=0, tq=_pick(t, (1280, 256)), tk=_pick(s, DENSE_KV_CHUNKS), ctx_rows=l,
                       window=False)
            y = _mm(o, mla_w_out[jdx].astype(BF16), F32)
        elif kind == 2:
            qkv = _mm_qkv(u, _gqa_weights(qkn_w_in[jdx], n_heads, n_kv), *rope_head,
                          _deinterleave_cols(qkn_g_q[jdx], 1, HEAD_DIM),
                          _deinterleave_cols(qkn_g_k[jdx], 1, HEAD_DIM), n_heads, n_kv, qk_norm=True)
            o = _flash(qkv, qkv, qkv, None, groups=n_kv, rep=rep, dq=HEAD_DIM, dv=HEAD_DIM, q_off=0,
                       k_off=n_heads, v_off=n_heads + n_kv, tq=Q_CHUNK, tk=_pick(s, DENSE_KV_CHUNKS),
                       ctx_rows=l, window=False)
            y = _mm(o, qkn_w_out[jdx].astype(BF16), F32)
        else:
            bg, p = _conv_in(u, conv_w_in[jdx].astype(BF16))
            a = _conv_apply(p, bg, conv_w[jdx], l)
            y = _mm(a, conv_w_out[jdx].astype(BF16), F32)

        moe = i % 2 == 1
        res = _resid(h, y, g_mix_post[i], mod, 2, l, nxt=(g_ffn_pre[i], mod, 3),
                     router=moe_router[f] if moe else None)
        h, u = res[0], res[1]

        if moe:
            z = _moe(u, res[2], moe_w_in[f].astype(BF16), moe_w_out[f].astype(BF16))
        else:
            z = _mm(_mm_swiglu(u, ffn_w_in[f].astype(BF16)), ffn_w_out[f].astype(BF16), F32)

        if last:
            h = _resid(h, z, g_ffn_post[i], mod, 5, l, out_rows=s)[0]
        else:
            h, u = _resid(h, z, g_ffn_post[i], mod, 5, l, nxt=(g_mix_pre[i + 1], mods[i + 1], 0))
    return h.reshape(1, s, d)
```

```python
import functools

import jax
import jax.numpy as jnp
from jax import lax
from jax.experimental import pallas as pl
from jax.experimental.pallas import tpu as pltpu

HEAD_DIM = 128
GRID_W = 64
QK_NOPE = 128
QK_ROPE = 64
V_HEAD = 128
WINDOW = 128
EPS = 1e-6
ROPE_THETA = 10000.0
NEG_INF = -1e30
LOG2E = 1.4426950408889634
LANES = 128
ROW_TILE = 256
VMEM_LIMIT = 52 * 1024 * 1024

F32 = jnp.float32
BF16 = jnp.bfloat16


def _pick(n, candidates):
    for c in candidates:
        if n % c == 0:
            return c
    raise ValueError(f"no tile for {n} in {candidates}")


def _params(sem):
    return pltpu.CompilerParams(dimension_semantics=sem, vmem_limit_bytes=VMEM_LIMIT)


def _dot(a, b):
    return jnp.dot(a, b, preferred_element_type=F32)


def _silu(x):
    return x / (1.0 + jnp.exp(-x))


def _rms(x, g):
    return x * lax.rsqrt(jnp.mean(x * x, axis=-1, keepdims=True) + EPS) * g


def _ada_kernel(cc_ref, w_ref, b_ref, o_ref):
    a = _silu(cc_ref[...]).astype(BF16)
    o_ref[0] = _dot(a, w_ref[0].astype(BF16)) + b_ref[0]


def _ada(cc, ada_w, ada_b):
    depth, d, n = ada_w.shape
    tn = _pick(n, (1024, 512, 256, 128))
    return pl.pallas_call(
        _ada_kernel,
        grid=(depth, n // tn),
        in_specs=[pl.BlockSpec((8, d), lambda l, j: (0, 0)),
                  pl.BlockSpec((1, d, tn), lambda l, j: (l, 0, j)),
                  pl.BlockSpec((1, 1, tn), lambda l, j: (l, 0, j))],
        out_specs=pl.BlockSpec((1, 8, tn), lambda l, j: (l, 0, j)),
        out_shape=jax.ShapeDtypeStruct((depth, 8, n), F32),
        compiler_params=_params(("parallel", "parallel")),
    )(cc, ada_w, ada_b.reshape(depth, 1, n))


def _mod_vec(mod_ref, is_ctx, k, d):
    return jnp.where(is_ctx, mod_ref[1:2, k * d:(k + 1) * d], mod_ref[0:1, k * d:(k + 1) * d])


def _modulate_kernel(h_ref, g_ref, mod_ref, u_ref, *, ctx_tiles, k0):
    d = h_ref.shape[-1]
    is_ctx = pl.program_id(0) < ctx_tiles
    shift = _mod_vec(mod_ref, is_ctx, k0, d)
    scale = _mod_vec(mod_ref, is_ctx, k0 + 1, d)
    u_ref[...] = (_rms(h_ref[...], g_ref[...]) * (1.0 + scale) + shift).astype(BF16)


def _modulate(h, g, mod, k0, ctx_rows):
    t, d = h.shape
    return pl.pallas_call(
        functools.partial(_modulate_kernel, ctx_tiles=ctx_rows // ROW_TILE, k0=k0),
        grid=(t // ROW_TILE,),
        in_specs=[pl.BlockSpec((ROW_TILE, d), lambda i: (i, 0)),
                  pl.BlockSpec((1, d), lambda i: (0, 0)),
                  pl.BlockSpec((8, 6 * d), lambda i: (0, 0))],
        out_specs=pl.BlockSpec((ROW_TILE, d), lambda i: (i, 0)),
        out_shape=jax.ShapeDtypeStruct((t, d), BF16),
        compiler_params=_params(("parallel",)),
    )(h, g.reshape(1, d), mod)


def _pack_bf16_pairs(x):
    half = x.shape[1] // 2
    lo = lax.bitcast_convert_type(x[:, :half].astype(BF16).astype(F32), jnp.uint32)
    hi = lax.bitcast_convert_type(x[:, half:].astype(BF16).astype(F32), jnp.uint32)
    return (hi & jnp.uint32(0xFFFF0000)) | (lo >> 16)


def _unpack_bf16_pairs(w):
    lo = lax.bitcast_convert_type(w << 16, F32)
    hi = lax.bitcast_convert_type(w & jnp.uint32(0xFFFF0000), F32)
    return lo, hi


def _top2_route(logits, n_experts):
    lane = lax.broadcasted_iota(jnp.int32, logits.shape, 1)
    lg = jnp.where(lane < n_experts, logits, -jnp.inf)
    m1 = jnp.max(lg, axis=1, keepdims=True)
    i1 = jnp.min(jnp.where(lg == m1, lane, LANES), axis=1, keepdims=True)
    lg2 = jnp.where(lane == i1, -jnp.inf, lg)
    m2 = jnp.max(lg2, axis=1, keepdims=True)
    i2 = jnp.min(jnp.where(lg2 == m2, lane, LANES), axis=1, keepdims=True)
    e2 = jnp.exp(m2 - m1)
    g1 = 1.0 / (1.0 + e2)
    g2 = e2 / (1.0 + e2)
    out = jnp.where(lane == 0, i1.astype(F32), 0.0) + jnp.where(lane == 1, i2.astype(F32), 0.0)
    return out + jnp.where(lane == 2, g1, 0.0) + jnp.where(lane == 3, g2, 0.0)


def _resid_kernel(*refs, ctx_tiles, tile_off, gate_k, next_k, n_experts, y_packed):
    h_ref, y_ref = refs[:2]
    pos = 2
    if y_packed:
        yb_ref = refs[pos]
        pos += 1
    gp_ref, mod_ref = refs[pos:pos + 2]
    pos += 2
    if next_k is not None:
        gn_ref, modn_ref = refs[pos:pos + 2]
        pos += 2
    if n_experts:
        router_ref = refs[pos]
        pos += 1
    outs = refs[pos:]
    d = h_ref.shape[-1]
    is_ctx = (pl.program_id(0) + tile_off) < ctx_tiles
    gate = _mod_vec(mod_ref, is_ctx, gate_k, d)
    if y_packed:
        a_lo, a_hi = _unpack_bf16_pairs(y_ref[...])
        b_lo, b_hi = _unpack_bf16_pairs(yb_ref[...])
        y = jnp.concatenate([a_lo + b_lo, a_hi + b_hi], axis=1)
    else:
        y = y_ref[...].astype(F32)
    h = h_ref[...] + gate * _rms(y, gp_ref[...])
    outs[0][...] = h
    if next_k is not None:
        shift = _mod_vec(modn_ref, is_ctx, next_k, d)
        scale = _mod_vec(modn_ref, is_ctx, next_k + 1, d)
        u = _rms(h, gn_ref[...]) * (1.0 + scale) + shift
        if n_experts:
            outs[1][...] = _pack_bf16_pairs(u)
            logits = jnp.dot(u, router_ref[...], preferred_element_type=F32,
                             precision=lax.Precision.HIGHEST)
            outs[2][...] = _top2_route(logits, n_experts)
        else:
            outs[1][...] = u.astype(BF16)


def _resid(h, y, g_post, mod, gate_k, ctx_rows, nxt=None, router=None, out_rows=None):
    t, d = h.shape
    off = 0 if out_rows is None else (t - out_rows) // ROW_TILE
    n_out = t if out_rows is None else out_rows
    row = lambda i: (i + off, 0)
    const = lambda i: (0, 0)
    y_packed = y.dtype == jnp.uint32
    if y_packed:
        y_args = [y, y]
        y_specs = [pl.BlockSpec((ROW_TILE, d // 2), row),
                   pl.BlockSpec((ROW_TILE, d // 2), lambda i: (i + off + t // ROW_TILE, 0))]
    else:
        y_args, y_specs = [y], [pl.BlockSpec((ROW_TILE, d), row)]
    args = [h, *y_args, g_post.reshape(1, d), mod]
    in_specs = [pl.BlockSpec((ROW_TILE, d), row), *y_specs,
                pl.BlockSpec((1, d), const), pl.BlockSpec((8, 6 * d), const)]
    out_shape = [jax.ShapeDtypeStruct((n_out, d), F32)]
    out_specs = [pl.BlockSpec((ROW_TILE, d), lambda i: (i, 0))]
    n_experts = 0
    if nxt is not None:
        g_pre, mod_next, k0 = nxt
        args += [g_pre.reshape(1, d), mod_next]
        in_specs += [pl.BlockSpec((1, d), const), pl.BlockSpec((8, 6 * d), const)]
        if router is None:
            out_shape.append(jax.ShapeDtypeStruct((t, d), BF16))
            out_specs.append(pl.BlockSpec((ROW_TILE, d), lambda i: (i, 0)))
        else:
            n_experts = router.shape[-1]
            args.append(jnp.pad(router, ((0, 0), (0, LANES - n_experts))))
            in_specs.append(pl.BlockSpec((d, LANES), const))
            out_shape += [jax.ShapeDtypeStruct((t, d // 2), jnp.uint32),
                          jax.ShapeDtypeStruct((t, LANES), F32)]
            out_specs += [pl.BlockSpec((ROW_TILE, d // 2), lambda i: (i, 0)),
                          pl.BlockSpec((ROW_TILE, LANES), lambda i: (i, 0))]
    return pl.pallas_call(
        functools.partial(_resid_kernel, ctx_tiles=ctx_rows // ROW_TILE, tile_off=off, gate_k=gate_k,
                          next_k=None if nxt is None else nxt[2], n_experts=n_experts,
                          y_packed=y_packed),
        grid=(n_out // ROW_TILE,),
        in_specs=in_specs, out_specs=out_specs, out_shape=out_shape,
        input_output_aliases={0: 0} if out_rows is None else {},
        compiler_params=_params(("parallel",)),
    )(*args)


MM_VMEM_BUDGET = 40 * 1024 * 1024


def _row_tile(t, k, tn, n_w=1, out_bytes=2):
    for tm in (1280, 1024, 640, 512, 256, 128):
        if t % tm:
            continue
        need = 2 * tm * k * 2 + n_w * 2 * k * tn * 2 + 2 * tm * tn * out_bytes + n_w * tm * tn * 4
        if need <= MM_VMEM_BUDGET:
            return tm
    return 0


def _mm_tiles(t, k, n, n_w=1, out_bytes=2, tn_max=512):
    best = (0, 0)
    for tn in (512, 256, 128):
        if tn > tn_max or n % tn:
            continue
        tm = _row_tile(t, k, tn, n_w, out_bytes)
        if tm > best[0]:
            best = (tm, tn)
    assert best[0], f"no matmul tiling for t={t} k={k} n={n}"
    return best


def _mm_kernel(a_ref, w_ref, o_ref):
    o_ref[...] = _dot(a_ref[...], w_ref[...]).astype(o_ref.dtype)


def _mm(a, w, out_dtype):
    t, k = a.shape
    n = w.shape[1]
    tm, tn = _mm_tiles(t, k, n, out_bytes=jnp.dtype(out_dtype).itemsize)
    return pl.pallas_call(
        _mm_kernel,
        grid=(t // tm, n // tn),
        in_specs=[pl.BlockSpec((tm, k), lambda i, j: (i, 0)),
                  pl.BlockSpec((k, tn), lambda i, j: (0, j))],
        out_specs=pl.BlockSpec((tm, tn), lambda i, j: (i, j)),
        out_shape=jax.ShapeDtypeStruct((t, n), out_dtype),
        compiler_params=_params(("parallel", "arbitrary")),
    )(a, w)


def _swiglu_kernel(a_ref, wg_ref, wu_ref, o_ref):
    a = a_ref[...]
    g = _dot(a, wg_ref[...])
    o_ref[...] = (_silu(g) * _dot(a, wu_ref[...])).astype(BF16)


def _mm_swiglu(a, w):
    t, k = a.shape
    f = w.shape[1] // 2
    tm, tn = _mm_tiles(t, k, f, n_w=2)
    nt = f // tn
    return pl.pallas_call(
        _swiglu_kernel,
        grid=(t // tm, nt),
        in_specs=[pl.BlockSpec((tm, k), lambda i, j: (i, 0)),
                  pl.BlockSpec((k, tn), lambda i, j: (0, j)),
                  pl.BlockSpec((k, tn), lambda i, j: (0, j + nt))],
        out_specs=pl.BlockSpec((tm, tn), lambda i, j: (i, j)),
        out_shape=jax.ShapeDtypeStruct((t, f), BF16),
        compiler_params=_params(("parallel", "arbitrary")),
    )(a, w, w)


EXPERT_TILE = 256
GATHER_ROWS = 512


def _route_plan(route, n_exp):
    t = route.shape[0]
    n_slots = 2 * t + n_exp * EXPERT_TILE
    n_slots += -n_slots % GATHER_ROWS
    e_flat = route[:, :2].astype(jnp.int32).reshape(-1)
    g_flat = route[:, 2:4].reshape(-1)
    onehot = (e_flat[:, None] == jnp.arange(n_exp, dtype=jnp.int32)[None, :]).astype(jnp.int32)
    csum = jnp.cumsum(onehot, axis=0)
    rank = jnp.sum(csum * onehot, axis=1) - 1
    padded = (csum[-1] + EXPERT_TILE - 1) // EXPERT_TILE * EXPERT_TILE
    g_end = jnp.cumsum(padded)
    slot = jnp.sum(onehot * (g_end - padded)[None, :], axis=1) + rank
    per_slot = jnp.stack([jnp.arange(2 * t, dtype=jnp.int32) // 2,
                          lax.bitcast_convert_type(g_flat, jnp.int32)], axis=1)
    per_slot = jnp.zeros((n_slots, 2), jnp.int32).at[slot].set(per_slot)
    src_token = per_slot[:, 0]
    slot_gate = lax.bitcast_convert_type(per_slot[:, 1], F32)
    tile_start = jnp.arange(n_slots // EXPERT_TILE, dtype=jnp.int32) * EXPERT_TILE
    tile_expert = jnp.minimum(jnp.sum(tile_start[:, None] >= g_end[None, :], axis=1), n_exp - 1)
    n_used = (g_end[-1] // EXPERT_TILE).reshape(1)
    return slot, src_token, slot_gate.reshape(n_slots, 1), tile_expert.astype(jnp.int32), n_used


def _gather_kernel(idx_ref, tab_ref, out_ref, sem):
    def row_copy(r):
        return pltpu.make_async_copy(tab_ref.at[pl.ds(idx_ref[0, 0, r], 1)],
                                     out_ref.at[pl.ds(r, 1)], sem)

    def issue(r, carry):
        row_copy(r).start()
        return carry

    def drain(r, carry):
        row_copy(r).wait()
        return carry

    lax.fori_loop(0, GATHER_ROWS, issue, 0, unroll=8)
    lax.fori_loop(0, GATHER_ROWS, drain, 0, unroll=8)


def _gather_rows(table, idx):
    n = idx.shape[0]
    assert n % GATHER_ROWS == 0 and table.dtype.itemsize == 4
    return pl.pallas_call(
        _gather_kernel,
        grid=(n // GATHER_ROWS,),
        in_specs=[pl.BlockSpec((1, 1, GATHER_ROWS), lambda i: (i, 0, 0), memory_space=pltpu.SMEM),
                  pl.BlockSpec(memory_space=pl.ANY)],
        out_specs=pl.BlockSpec((GATHER_ROWS, table.shape[1]), lambda i: (i, 0)),
        out_shape=jax.ShapeDtypeStruct((n, table.shape[1]), table.dtype),
        scratch_shapes=[pltpu.SemaphoreType.DMA(())],
        compiler_params=_params(("arbitrary",)),
        name="gather_rows",
    )(idx.reshape(n // GATHER_ROWS, 1, GATHER_ROWS), table)


def _expert_in_kernel(te_ref, nu_ref, x_ref, wg_ref, wu_ref, gate_ref, o_ref):
    @pl.when(pl.program_id(0) < nu_ref[0])
    def _():
        half = wg_ref.shape[1] // 2
        lo, hi = _unpack_bf16_pairs(x_ref[...])
        lo, hi = lo.astype(BF16), hi.astype(BF16)
        g = _dot(lo, wg_ref[0, :half, :]) + _dot(hi, wg_ref[0, half:, :])
        up = _dot(lo, wu_ref[0, :half, :]) + _dot(hi, wu_ref[0, half:, :])
        o_ref[...] = (_silu(g) * up * gate_ref[...]).astype(BF16)

    @pl.when(pl.program_id(0) >= nu_ref[0])
    def _():
        o_ref[...] = jnp.zeros(o_ref.shape, BF16)


def _expert_in(x_slots, w_in, slot_gate, tile_expert, n_used):
    n_slots, half = x_slots.shape
    n_exp, k, f2 = w_in.shape
    f = f2 // 2
    grid_spec = pltpu.PrefetchScalarGridSpec(
        num_scalar_prefetch=2,
        grid=(n_slots // EXPERT_TILE,),
        in_specs=[pl.BlockSpec((EXPERT_TILE, half), lambda m, te, nu: (m, 0)),
                  pl.BlockSpec((1, k, f), lambda m, te, nu: (te[m], 0, 0)),
                  pl.BlockSpec((1, k, f), lambda m, te, nu: (te[m], 0, 1)),
                  pl.BlockSpec((EXPERT_TILE, 1), lambda m, te, nu: (m, 0))],
        out_specs=pl.BlockSpec((EXPERT_TILE, f), lambda m, te, nu: (m, 0)))
    return pl.pallas_call(
        _expert_in_kernel, grid_spec=grid_spec,
        out_shape=jax.ShapeDtypeStruct((n_slots, f), BF16),
        compiler_params=_params(("arbitrary",)),
        name="expert_in",
    )(tile_expert, n_used, x_slots, w_in, w_in, slot_gate)


def _expert_out_kernel(te_ref, nu_ref, a_ref, w_ref, o_ref):
    @pl.when(pl.program_id(0) < nu_ref[0])
    def _():
        o_ref[...] = _pack_bf16_pairs(_dot(a_ref[...], w_ref[0]))

    @pl.when(pl.program_id(0) >= nu_ref[0])
    def _():
        o_ref[...] = jnp.zeros(o_ref.shape, jnp.uint32)


def _expert_out(act, w_out, tile_expert, n_used):
    n_slots, f = act.shape
    d = w_out.shape[2]
    grid_spec = pltpu.PrefetchScalarGridSpec(
        num_scalar_prefetch=2,
        grid=(n_slots // EXPERT_TILE,),
        in_specs=[pl.BlockSpec((EXPERT_TILE, f), lambda m, te, nu: (m, 0)),
                  pl.BlockSpec((1, f, d), lambda m, te, nu: (te[m], 0, 0))],
        out_specs=pl.BlockSpec((EXPERT_TILE, d // 2), lambda m, te, nu: (m, 0)))
    return pl.pallas_call(
        _expert_out_kernel, grid_spec=grid_spec,
        out_shape=jax.ShapeDtypeStruct((n_slots, d // 2), jnp.uint32),
        compiler_params=_params(("arbitrary",)),
        name="expert_out",
    )(tile_expert, n_used, act, w_out)


def _moe(u_packed, route, w_in, w_out):
    t = u_packed.shape[0]
    assert (2 * t) % GATHER_ROWS == 0
    slot, src_token, slot_gate, tile_expert, n_used = _route_plan(route, w_in.shape[0])
    x_slots = _gather_rows(u_packed, src_token)
    y_slots = _expert_out(_expert_in(x_slots, w_in, slot_gate, tile_expert, n_used), w_out,
                          tile_expert, n_used)
    return _gather_rows(y_slots, jnp.concatenate([slot[0::2], slot[1::2]]))


def _rope128(x, cos2, sin2):
    return x * cos2 + pltpu.roll(x, 64, 1) * sin2


def _rope64(x, cosp, sa, sb):
    return x * cosp + pltpu.roll(x, 32, 1) * sa + pltpu.roll(x, 96, 1) * sb


def _qkv_kernel(a_ref, w_ref, cos_ref, sin_ref, gq_ref, gk_ref, o_ref, *, q_tiles, k_tiles,
                q_scale, qk_norm):
    j = pl.program_id(1)
    acc = _dot(a_ref[...], w_ref[...])

    @pl.when(j < q_tiles + k_tiles)
    def _():
        is_q = j < q_tiles
        cos2, sin2 = cos_ref[...], sin_ref[...]
        g = jnp.where(is_q, gq_ref[...], gk_ref[...])
        sc = jnp.where(is_q, q_scale, 1.0)
        outs = []
        for hh in range(acc.shape[1] // HEAD_DIM):
            x = acc[:, hh * HEAD_DIM:(hh + 1) * HEAD_DIM]
            if qk_norm:
                x = _rms(x, g)
            outs.append((_rope128(x, cos2, sin2) * sc).astype(BF16))
        o_ref[...] = jnp.concatenate(outs, axis=1)

    @pl.when(j >= q_tiles + k_tiles)
    def _():
        o_ref[...] = acc.astype(BF16)


def _mm_qkv(a, w, cos2, sin2, gq, gk, n_heads, n_kv, qk_norm):
    t, k = a.shape
    n = w.shape[1]
    tm, tn = _mm_tiles(t, k, n_kv * HEAD_DIM)
    const = lambda i, j: (0, 0)
    return pl.pallas_call(
        functools.partial(_qkv_kernel, q_tiles=n_heads * HEAD_DIM // tn, k_tiles=n_kv * HEAD_DIM // tn,
                          q_scale=HEAD_DIM ** -0.5 * LOG2E, qk_norm=qk_norm),
        grid=(t // tm, n // tn),
        in_specs=[pl.BlockSpec((tm, k), lambda i, j: (i, 0)),
                  pl.BlockSpec((k, tn), lambda i, j: (0, j)),
                  pl.BlockSpec((tm, HEAD_DIM), lambda i, j: (i, 0)),
                  pl.BlockSpec((tm, HEAD_DIM), lambda i, j: (i, 0)),
                  pl.BlockSpec((1, HEAD_DIM), const),
                  pl.BlockSpec((1, HEAD_DIM), const)],
        out_specs=pl.BlockSpec((tm, tn), lambda i, j: (i, j)),
        out_shape=jax.ShapeDtypeStruct((t, n), BF16),
        compiler_params=_params(("parallel", "arbitrary")),
    )(a, w, cos2, sin2, gq.reshape(1, HEAD_DIM), gk.reshape(1, HEAD_DIM))


Q_CHUNK = 256
DENSE_KV_CHUNKS = (2048, 1024, 512, 256)


def _flash_kernel(q_ref, k_ref, v_ref, sink_ref, o_ref, m_ref, l_ref, acc_ref, *, rep, dq, dv, tq,
                  tk, ctx_rows, window, use_sink):
    g, i = pl.program_id(0), pl.program_id(1)
    t = k_ref.shape[0]
    n_sub = tq // Q_CHUNK
    chains = [(r, c) for r in range(rep) for c in range(n_sub)]
    row0 = i * tq

    for ch, (r, _) in enumerate(chains):
        if use_sink:
            m_ref[ch] = jnp.full((Q_CHUNK, LANES), sink_ref[g * rep + r] * LOG2E, F32)
            l_ref[ch] = jnp.full((Q_CHUNK, LANES), 1.0 / LANES, F32)
        else:
            m_ref[ch] = jnp.full((Q_CHUNK, LANES), NEG_INF, F32)
            l_ref[ch] = jnp.zeros((Q_CHUNK, LANES), F32)
        acc_ref[ch] = jnp.zeros((Q_CHUNK, dv), F32)

    def scores(ch, k):
        r, c = chains[ch]
        q = q_ref[c * Q_CHUNK:(c + 1) * Q_CHUNK, r * dq:(r + 1) * dq]
        return lax.dot_general(q, k, (((1,), (1,)), ((), ())), preferred_element_type=F32)

    def update(ch, s, v):
        cols = [s[:, j * LANES:(j + 1) * LANES] for j in range(s.shape[1] // LANES)]
        m_prev = m_ref[ch]
        m_new = jnp.maximum(m_prev, jnp.max(functools.reduce(jnp.maximum, cols), axis=1, keepdims=True))
        alpha = jnp.exp2(m_prev - m_new)
        ps = [jnp.exp2(cj - m_new) for cj in cols]
        l_ref[ch] = alpha * l_ref[ch] + functools.reduce(jnp.add, ps)
        p = jnp.concatenate([x.astype(BF16) for x in ps], axis=1)
        acc_ref[ch] = alpha * acc_ref[ch] + _dot(p, v)
        m_ref[ch] = m_new

    def block(k, v, fix):
        s_next = scores(0, k)
        for ch in range(len(chains)):
            s = s_next
            if ch + 1 < len(chains):
                s_next = scores(ch + 1, k)
            update(ch, fix(ch, s), v)

    block(k_ref[0:ctx_rows, :], v_ref[0:ctx_rows, :], lambda ch, s: s)

    if window:
        span = tq + 2 * WINDOW

        @pl.when(row0 >= ctx_rows)
        def _():
            start = pl.multiple_of(jnp.clip(row0 - WINDOW, ctx_rows, t - span), WINDOW)
            qpos = row0 + lax.broadcasted_iota(jnp.int32, (tq, span), 0)
            kpos = start + lax.broadcasted_iota(jnp.int32, (tq, span), 1)
            valid = jnp.abs(kpos - qpos) <= WINDOW
            block(k_ref[pl.ds(start, span), :], v_ref[pl.ds(start, span), :],
                  lambda ch, s: jnp.where(valid, s, NEG_INF))
    else:
        pure = ctx_rows % tq == 0
        n_lat = (t - ctx_rows) // tk

        def fix(ch, s):
            c = chains[ch][1]
            if pure or c * Q_CHUNK >= ctx_rows:
                return s
            return jnp.where(row0 + c * Q_CHUNK < ctx_rows, NEG_INF, s)

        def body(tc, carry):
            start = pl.multiple_of(ctx_rows + tc * tk, Q_CHUNK)
            block(k_ref[pl.ds(start, tk), :], v_ref[pl.ds(start, tk), :], fix)
            return carry

        skip = jnp.logical_and(pure, row0 < ctx_rows)
        lax.fori_loop(0, jnp.where(skip, 0, n_lat // 2),
                      lambda t2, carry: body(2 * t2 + 1, body(2 * t2, carry)), 0)
        if n_lat % 2:
            @pl.when(jnp.logical_not(skip))
            def _():
                body(n_lat - 1, 0)

    for ch, (r, c) in enumerate(chains):
        o = acc_ref[ch] / jnp.sum(l_ref[ch], axis=1, keepdims=True)
        o_ref[c * Q_CHUNK:(c + 1) * Q_CHUNK, r * dv:(r + 1) * dv] = o.astype(BF16)


def _flash(q_arr, k_arr, v_arr, sink, *, groups, rep, dq, dv, q_off, k_off, v_off, tq, tk,
           ctx_rows, window):
    t = q_arr.shape[0]
    assert tq % Q_CHUNK == 0 and t % tq == 0 and ctx_rows % Q_CHUNK == 0
    if window:
        assert tq == Q_CHUNK and ctx_rows % tq == 0 and t - ctx_rows >= tq + 2 * WINDOW
    else:
        assert (t - ctx_rows) % tk == 0 and tk % Q_CHUNK == 0
    use_sink = sink is not None
    if sink is None:
        sink = jnp.zeros((groups * rep,), F32)
    n_chains = rep * (tq // Q_CHUNK)
    return pl.pallas_call(
        functools.partial(_flash_kernel, rep=rep, dq=dq, dv=dv, tq=tq, tk=tk, ctx_rows=ctx_rows,
                          window=window, use_sink=use_sink),
        grid=(groups, t // tq),
        in_specs=[pl.BlockSpec((tq, rep * dq), lambda g, i: (i, q_off // rep + g)),
                  pl.BlockSpec((t, dq), lambda g, i: (0, k_off + g)),
                  pl.BlockSpec((t, dv), lambda g, i: (0, v_off + g)),
                  pl.BlockSpec(memory_space=pltpu.SMEM)],
        out_specs=pl.BlockSpec((tq, rep * dv), lambda g, i: (i, g)),
        out_shape=jax.ShapeDtypeStruct((t, groups * rep * dv), BF16),
        scratch_shapes=[pltpu.VMEM((n_chains, Q_CHUNK, LANES), F32),
                        pltpu.VMEM((n_chains, Q_CHUNK, LANES), F32),
                        pltpu.VMEM((n_chains, Q_CHUNK, dv), F32)],
        compiler_params=_params(("parallel", "arbitrary")),
        name=f"flash_{'window' if window else 'dense'}_dq{dq}",
    )(q_arr, k_arr, v_arr, sink)


def _mla_in_kernel(a_ref, w_ref, gq_ref, gkv_ref, cos_ref, sa_ref, sb_ref, cq_ref, ckv_ref, kr_ref,
                   *, q_lora, kv_lora):
    acc = _dot(a_ref[...], w_ref[...])
    cq_ref[...] = _rms(acc[:, :q_lora], gq_ref[...]).astype(BF16)
    ckv_ref[...] = _rms(acc[:, q_lora:q_lora + kv_lora], gkv_ref[...]).astype(BF16)
    kr = acc[:, q_lora + kv_lora:]
    kr_ref[...] = _rope64(kr, cos_ref[...], sa_ref[...], sb_ref[...]).astype(BF16)


def _mla_in(a, w, g_q, g_kv, tabs):
    t, k = a.shape
    n = w.shape[1]
    q_lora, kv_lora = g_q.shape[0], g_kv.shape[0]
    tm = _pick(t, (256, 128))
    const = lambda i: (0, 0)
    row = lambda i: (i, 0)
    return pl.pallas_call(
        functools.partial(_mla_in_kernel, q_lora=q_lora, kv_lora=kv_lora),
        grid=(t // tm,),
        in_specs=[pl.BlockSpec((tm, k), row), pl.BlockSpec((k, n), const),
                  pl.BlockSpec((1, q_lora), const), pl.BlockSpec((1, kv_lora), const),
                  pl.BlockSpec((tm, LANES), row), pl.BlockSpec((tm, LANES), row),
                  pl.BlockSpec((tm, LANES), row)],
        out_specs=[pl.BlockSpec((tm, q_lora), row), pl.BlockSpec((tm, kv_lora), row),
                   pl.BlockSpec((tm, LANES), row)],
        out_shape=[jax.ShapeDtypeStruct((t, q_lora), BF16), jax.ShapeDtypeStruct((t, kv_lora), BF16),
                   jax.ShapeDtypeStruct((t, LANES), BF16)],
        compiler_params=_params(("parallel",)),
    )(a, w, g_q.reshape(1, -1), g_kv.reshape(1, -1), *tabs)


def _mla_q_kernel(a_ref, w_ref, cos_ref, sa_ref, sb_ref, o_ref, *, q_scale):
    acc = _dot(a_ref[...], w_ref[...])
    cosp, sa, sb = cos_ref[...], sa_ref[...], sb_ref[...]
    outs = []
    for hh in range(acc.shape[1] // 256):
        outs.append((acc[:, hh * 256:hh * 256 + 128] * q_scale).astype(BF16))
        outs.append((_rope64(acc[:, hh * 256 + 128:(hh + 1) * 256], cosp, sa, sb) * q_scale).astype(BF16))
    o_ref[...] = jnp.concatenate(outs, axis=1)


def _mla_q(a, w, tabs):
    t, k = a.shape
    n = w.shape[1]
    tm, tn = _row_tile(t, k, 512), 512
    row = lambda i, j: (i, 0)
    return pl.pallas_call(
        functools.partial(_mla_q_kernel, q_scale=(QK_NOPE + QK_ROPE) ** -0.5 * LOG2E),
        grid=(t // tm, n // tn),
        in_specs=[pl.BlockSpec((tm, k), row), pl.BlockSpec((k, tn), lambda i, j: (0, j)),
                  pl.BlockSpec((tm, LANES), row), pl.BlockSpec((tm, LANES), row),
                  pl.BlockSpec((tm, LANES), row)],
        out_specs=pl.BlockSpec((tm, tn), lambda i, j: (i, j)),
        out_shape=jax.ShapeDtypeStruct((t, n), BF16),
        compiler_params=_params(("parallel", "arbitrary")),
    )(a, w, *tabs)


def _mla_kv_kernel(a_ref, w_ref, kr_ref, k_ref, v_ref):
    acc = _dot(a_ref[...], w_ref[...])
    kr = kr_ref[...]
    ks, vs = [], []
    for hh in range(acc.shape[1] // 256):
        ks += [acc[:, hh * 256:hh * 256 + 128].astype(BF16), kr]
        vs.append(acc[:, hh * 256 + 128:(hh + 1) * 256].astype(BF16))
    k_ref[...] = jnp.concatenate(ks, axis=1)
    v_ref[...] = jnp.concatenate(vs, axis=1) if len(vs) > 1 else vs[0]


def _mla_kv(a, w, kr):
    t, k = a.shape
    n = w.shape[1]
    tm, tn = _row_tile(t, k, 512), 512
    row = lambda i, j: (i, 0)
    return pl.pallas_call(
        _mla_kv_kernel,
        grid=(t // tm, n // tn),
        in_specs=[pl.BlockSpec((tm, k), row), pl.BlockSpec((k, tn), lambda i, j: (0, j)),
                  pl.BlockSpec((tm, LANES), row)],
        out_specs=[pl.BlockSpec((tm, tn), lambda i, j: (i, j)),
                   pl.BlockSpec((tm, tn // 2), lambda i, j: (i, j))],
        out_shape=[jax.ShapeDtypeStruct((t, n), BF16), jax.ShapeDtypeStruct((t, n // 2), BF16)],
        compiler_params=_params(("parallel", "arbitrary")),
    )(a, w, kr)


def _conv_in_kernel(a_ref, wb_ref, wc_ref, wv_ref, b_ref, p_ref):
    a = a_ref[...]
    b_ref[...] = _dot(a, wb_ref[...]).astype(BF16)
    p_ref[...] = (_dot(a, wc_ref[...]) * _dot(a, wv_ref[...])).astype(BF16)


def _conv_in(a, w):
    t, k = a.shape
    d = w.shape[1] // 3
    tm, tn = _mm_tiles(t, k, d, n_w=3, out_bytes=4)
    nt = d // tn
    return pl.pallas_call(
        _conv_in_kernel,
        grid=(t // tm, nt),
        in_specs=[pl.BlockSpec((tm, k), lambda i, j: (i, 0)),
                  pl.BlockSpec((k, tn), lambda i, j: (0, j)),
                  pl.BlockSpec((k, tn), lambda i, j: (0, j + nt)),
                  pl.BlockSpec((k, tn), lambda i, j: (0, j + 2 * nt))],
        out_specs=[pl.BlockSpec((tm, tn), lambda i, j: (i, j))] * 2,
        out_shape=[jax.ShapeDtypeStruct((t, d), BF16)] * 2,
        compiler_params=_params(("parallel", "arbitrary")),
    )(a, w, w, w)


HALO = 16


def _conv_apply_kernel(p_ref, prev_ref, next_ref, b_ref, w_ref, o_ref, *, ctx_rows, total_rows):
    tr = p_ref.shape[0]
    p = p_ref[...].astype(F32)
    r = pl.program_id(0) * tr + lax.broadcasted_iota(jnp.int32, (tr, 1), 0)
    first = lax.broadcasted_iota(jnp.int32, (tr, 1), 0) == 0
    last = lax.broadcasted_iota(jnp.int32, (tr, 1), 0) == tr - 1
    up = jnp.where(first, prev_ref[HALO - 1:HALO, :].astype(F32), pltpu.roll(p, 1, 0))
    dn = jnp.where(last, next_ref[0:1, :].astype(F32), pltpu.roll(p, tr - 1, 0))
    up = jnp.where(jnp.logical_or(r == 0, r == ctx_rows), 0.0, up)
    dn = jnp.where(jnp.logical_or(r == ctx_rows - 1, r == total_rows - 1), 0.0, dn)
    z = w_ref[0:1, :] * up + w_ref[1:2, :] * p + w_ref[2:3, :] * dn
    o_ref[...] = (b_ref[...].astype(F32) * z).astype(BF16)


def _conv_apply(p, b, conv_w, ctx_rows):
    t, d = p.shape
    tr = ROW_TILE
    hb = tr // HALO
    n_halo = t // HALO
    return pl.pallas_call(
        functools.partial(_conv_apply_kernel, ctx_rows=ctx_rows, total_rows=t),
        grid=(t // tr,),
        in_specs=[pl.BlockSpec((tr, d), lambda i: (i, 0)),
                  pl.BlockSpec((HALO, d), lambda i: (jnp.maximum(i * hb - 1, 0), 0)),
                  pl.BlockSpec((HALO, d), lambda i: (jnp.minimum((i + 1) * hb, n_halo - 1), 0)),
                  pl.BlockSpec((tr, d), lambda i: (i, 0)),
                  pl.BlockSpec((8, d), lambda i: (0, 0))],
        out_specs=pl.BlockSpec((tr, d), lambda i: (i, 0)),
        out_shape=jax.ShapeDtypeStruct((t, d), BF16),
        compiler_params=_params(("parallel",)),
    )(p, p, p, b, jnp.pad(conv_w, ((0, 8 - conv_w.shape[0]), (0, 0))))


def _deinterleave_cols(w, n_blocks, width):
    lead = w.shape[:-1]
    return w.reshape(lead + (n_blocks, width // 2, 2)).swapaxes(-1, -2).reshape(lead + (n_blocks * width,))


def _rope_tables(seq, ctx_rows):
    rows = seq // GRID_W
    row = jnp.repeat(jnp.arange(rows, dtype=F32), GRID_W)
    col = jnp.tile(jnp.arange(GRID_W, dtype=F32), rows)

    def cs(rot_dim):
        n_freq = rot_dim // 4
        inv = ROPE_THETA ** (-jnp.arange(n_freq, dtype=F32) / n_freq)
        ang = jnp.concatenate([row[:, None] * inv[None, :], col[:, None] * inv[None, :]], axis=-1)
        half = rot_dim // 2
        cos = jnp.concatenate([jnp.ones((ctx_rows, half), F32), jnp.cos(ang)], axis=0)
        sin = jnp.concatenate([jnp.zeros((ctx_rows, half), F32), jnp.sin(ang)], axis=0)
        return cos, sin

    cos, sin = cs(HEAD_DIM)
    head = (jnp.concatenate([cos, cos], axis=1), jnp.concatenate([-sin, sin], axis=1))
    cos, sin = cs(QK_ROPE)
    z = jnp.zeros_like(cos)
    mla = (jnp.concatenate([cos, cos, z, z], axis=1), jnp.concatenate([z, sin, z, z], axis=1),
           jnp.concatenate([-sin, z, z, z], axis=1))
    return head, mla


def _gqa_weights(w_in, n_heads, n_kv):
    nq, nk = n_heads * HEAD_DIM, n_kv * HEAD_DIM
    return jnp.concatenate([_deinterleave_cols(w_in[:, :nq], n_heads, HEAD_DIM),
                            _deinterleave_cols(w_in[:, nq:nq + nk], n_kv, HEAD_DIM),
                            w_in[:, nq + nk:]], axis=1).astype(BF16)


def kernel(x, c, ctx, c_ctx, ada_w, ada_b, g_mix_pre, g_mix_post, g_ffn_pre, g_ffn_post, win_w_in, win_sink, win_w_out, mla_w_in, mla_g_q, mla_g_kv, mla_w_uq, mla_w_ukv, mla_w_out, qkn_w_in, qkn_g_q, qkn_g_k, qkn_w_out, conv_w_in, conv_w, conv_w_out, ffn_w_in, ffn_w_out, moe_router, moe_w_in, moe_w_out):
    b, s, d = x.shape
    assert b == 1 and c.shape[0] == 1, "one sequence per call"
    l = ctx.shape[1]
    t = l + s
    depth = ada_w.shape[0]
    n_heads = d // HEAD_DIM
    n_kv = (win_w_in.shape[-1] // HEAD_DIM - n_heads) // 2
    rep = n_heads // n_kv
    assert l % ROW_TILE == 0 and s % ROW_TILE == 0 and s % GRID_W == 0

    cc = jnp.zeros((8, d), F32).at[0].set(c[0]).at[1].set(c_ctx)
    mods = _ada(cc, ada_w, ada_b)
    rope_head, rope_mla = _rope_tables(s, l)
    ones_h = jnp.ones((HEAD_DIM,), F32)

    h = jnp.concatenate([ctx[0], x[0]], axis=0)
    u = _modulate(h, g_mix_pre[0], mods[0], 0, l)
    for i in range(depth):
        kind, jdx, f = i % 4, i // 4, i // 2
        last = i == depth - 1
        mod = mods[i]

        if kind == 0:
            qkv = _mm_qkv(u, _gqa_weights(win_w_in[jdx], n_heads, n_kv), *rope_head, ones_h, ones_h,
                          n_heads, n_kv, qk_norm=False)
            o = _flash(qkv, qkv, qkv, win_sink[jdx], groups=n_kv, rep=rep, dq=HEAD_DIM, dv=HEAD_DIM,
                       q_off=0, k_off=n_heads, v_off=n_heads + n_kv, tq=Q_CHUNK, tk=0, ctx_rows=l,
                       window=True)
            y = _mm(o, win_w_out[jdx].astype(BF16), BF16)
        elif kind == 1:
            q_lora, kv_lora = mla_g_q.shape[-1], mla_g_kv.shape[-1]
            w_in = mla_w_in[jdx]
            w_in = jnp.concatenate(
                [w_in[:, :q_lora + kv_lora], _deinterleave_cols(w_in[:, q_lora + kv_lora:], 1, QK_ROPE),
                 jnp.zeros((d, LANES - QK_ROPE), F32)], axis=1).astype(BF16)
            w_uq = mla_w_uq[jdx].reshape(q_lora, n_heads, QK_NOPE + QK_ROPE)
            w_uq = jnp.concatenate(
                [w_uq[..., :QK_NOPE], _deinterleave_cols(w_uq[..., QK_NOPE:], 1, QK_ROPE),
                 jnp.zeros((q_lora, n_heads, LANES - QK_ROPE), F32)], axis=-1)
            w_uq = w_uq.reshape(q_lora, n_heads * 256).astype(BF16)
            cq, ckv, kr = _mla_in(u, w_in, mla_g_q[jdx], mla_g_kv[jdx], rope_mla)
            q = _mla_q(cq, w_uq, rope_mla)
            k, v = _mla_kv(ckv, mla_w_ukv[jdx].astype(BF16), kr)
            o = _flash(q, k, v, None, groups=n_heads, rep=1, dq=256, dv=V_HEAD, q_off=0, k_off=0,
                       v_off=0, tq=_pick(t, (1280, 256)), tk=_pick(s, DENSE_KV_CHUNKS), ctx_rows=l,
                       window=False)
            y = _mm(o, mla_w_out[jdx].astype(BF16), BF16)
        elif kind == 2:
            qkv = _mm_qkv(u, _gqa_weights(qkn_w_in[jdx], n_heads, n_kv), *rope_head,
                          _deinterleave_cols(qkn_g_q[jdx], 1, HEAD_DIM),
                          _deinterleave_cols(qkn_g_k[jdx], 1, HEAD_DIM), n_heads, n_kv, qk_norm=True)
            o = _flash(qkv, qkv, qkv, None, groups=n_kv, rep=rep, dq=HEAD_DIM, dv=HEAD_DIM, q_off=0,
                       k_off=n_heads, v_off=n_heads + n_kv, tq=Q_CHUNK, tk=_pick(s, DENSE_KV_CHUNKS),
                       ctx_rows=l, window=False)
            y = _mm(o, qkn_w_out[jdx].astype(BF16), BF16)
        else:
            bg, p = _conv_in(u, conv_w_in[jdx].astype(BF16))
            a = _conv_apply(p, bg, conv_w[jdx], l)
            y = _mm(a, conv_w_out[jdx].astype(BF16), BF16)

        moe = i % 2 == 1
        res = _resid(h, y, g_mix_post[i], mod, 2, l, nxt=(g_ffn_pre[i], mod, 3),
                     router=moe_router[f] if moe else None)
        h, u = res[0], res[1]

        if moe:
            z = _moe(u, res[2], moe_w_in[f].astype(BF16), moe_w_out[f].astype(BF16))
        else:
            z = _mm(_mm_swiglu(u, ffn_w_in[f].astype(BF16)), ffn_w_out[f].astype(BF16), BF16)

        if last:
            h = _resid(h, z, g_ffn_post[i], mod, 5, l, out_rows=s)[0]
        else:
            h, u = _resid(h, z, g_ffn_post[i], mod, 5, l, nxt=(g_mix_pre[i + 1], mods[i + 1], 0))
    return h.reshape(1, s, d)
```
